```python
import jax, jax.numpy as jnp
from jax import lax
import numpy as np

D_MODEL = 1024
BATCH = 4
SEQ = 8192
DEPTH = 1
DEC_BATCH = 2
DEC_SEQ = 16384
PAST_LEN = 128

GRID_W = 64
HEAD_DIM = 64
NA_HEADS = 8
NA_KR = 8
NA_KC = 16
NA_WIDTH = NA_HEADS * HEAD_DIM
MLA_HEADS = 8
MLA_NOPE = 64
MLA_ROPE = 32
MLA_V = 64
MLA_QK = MLA_NOPE + MLA_ROPE
Q_LORA = 384
KV_LORA = 256
MLA_WIDTH = MLA_HEADS * MLA_V
MIX_WIDTH = NA_WIDTH + MLA_WIDTH
IN_WIDTH = 3 * NA_WIDTH + Q_LORA + KV_LORA + MLA_ROPE
ROPE_THETA = 10000.0
Q_BLOCK = 128
N_EXPERTS = 32
TOP_K = 4
D_FF = D_MODEL
SWIGLU_LIMIT = 7.0
SWIGLU_ALPHA = 1.702
MOE_BLOCK = 128
EPS = 1e-6

kernel_name = "hybrid_na_mla_moe_encoder"


def rms_norm(x, g):
    xf = x.astype(jnp.float32)
    y = xf * lax.rsqrt(jnp.mean(xf * xf, axis=-1, keepdims=True) + EPS)
    return (y * g.astype(jnp.float32)).astype(x.dtype)


def rope_tables(S):
    half = MLA_ROPE // 2
    inv = ROPE_THETA ** (-jnp.arange(half, dtype=jnp.float32) / half)
    ang = jnp.arange(S, dtype=jnp.float32)[:, None] * inv[None, :]
    return jnp.cos(ang)[:, None, :], jnp.sin(ang)[:, None, :]


def apply_rope(x, cos, sin):
    half = MLA_ROPE // 2
    x1, x2 = x[..., :half], x[..., half:]
    out = jnp.concatenate([x1 * cos - x2 * sin, x2 * cos + x1 * sin], axis=-1)
    return out.astype(x.dtype)


def neighbourhood_attention(q, k, v, rpb):
    B, S, H, dh = q.shape
    rows = S // GRID_W
    kr = min(NA_KR, rows)
    scale = 1.0 / float(np.sqrt(dh))
    qg = q.reshape(B, rows, GRID_W, H, dh)
    kg = k.reshape(B, rows, GRID_W, H, dh)
    vg = v.reshape(B, rows, GRID_W, H, dh)
    cols = np.arange(GRID_W)
    cstart = np.clip(cols - NA_KC // 2, 0, GRID_W - NA_KC)
    col_idx_np = cstart[:, None] + np.arange(NA_KC)[None, :]
    col_idx = jnp.asarray(col_idx_np, dtype=jnp.int32)
    col_rel = jnp.asarray(col_idx_np - cols[:, None] + NA_KC - 1, dtype=jnp.int32)

    def one_row(args):
        r, q_row = args
        rs = jnp.clip(r - kr // 2, 0, rows - kr)
        k_rows = lax.dynamic_slice_in_dim(kg, rs, kr, axis=1)
        v_rows = lax.dynamic_slice_in_dim(vg, rs, kr, axis=1)
        k_win = k_rows[:, :, col_idx]
        v_win = v_rows[:, :, col_idx]
        row_rel = rs + jnp.arange(kr, dtype=jnp.int32) - r + NA_KR - 1
        bias = rpb[:, row_rel[None, :, None], col_rel[:, None, :]]
        s = jnp.einsum('bchd,bicjhd->bhcij', q_row, k_win).astype(jnp.float32) * scale
        s = s + bias.astype(jnp.float32)[None]
        p = jax.nn.softmax(s.reshape(B, H, GRID_W, kr * NA_KC), axis=-1)
        p = p.reshape(B, H, GRID_W, kr, NA_KC).astype(v.dtype)
        return jnp.einsum('bhcij,bicjhd->bchd', p, v_win)

    o = lax.map(one_row, (jnp.arange(rows, dtype=jnp.int32), qg.transpose(1, 0, 2, 3, 4)))
    return o.transpose(1, 0, 2, 3, 4).reshape(B, S, H * dh)


def block_attention(q, k, v):
    B, S, H, DK = q.shape
    DV = v.shape[-1]
    nb = S // Q_BLOCK
    scale = 1.0 / float(np.sqrt(DK))
    qb = q.reshape(B, nb, Q_BLOCK, H, DK).transpose(1, 0, 2, 3, 4)

    def one(q_blk):
        s = jnp.einsum('bqhd,bkhd->bhqk', q_blk, k).astype(jnp.float32) * scale
        p = jax.nn.softmax(s, axis=-1).astype(v.dtype)
        return jnp.einsum('bhqk,bkhd->bqhd', p, v)

    o = lax.map(one, qb)
    return o.transpose(1, 0, 2, 3, 4).reshape(B, S, H * DV)


def latent_attention(cq, ckv, k_rope_in, q_a_norm, w_q_b, kv_a_norm, w_kv_b):
    B, S, _ = cq.shape
    q = (rms_norm(cq, q_a_norm) @ w_q_b).reshape(B, S, MLA_HEADS, MLA_QK)
    kv = (rms_norm(ckv, kv_a_norm) @ w_kv_b).reshape(B, S, MLA_HEADS, MLA_NOPE + MLA_V)
    cos, sin = rope_tables(S)
    q_pe = apply_rope(q[..., MLA_NOPE:], cos, sin)
    k_pe = apply_rope(k_rope_in[:, :, None, :], cos, sin)
    q_full = jnp.concatenate([q[..., :MLA_NOPE], q_pe], axis=-1)
    k_full = jnp.concatenate([kv[..., :MLA_NOPE],
                              jnp.broadcast_to(k_pe, (B, S, MLA_HEADS, MLA_ROPE))], axis=-1)
    return block_attention(q_full, k_full, kv[..., MLA_NOPE:])


def token_mixer(h, p):
    B, S, _ = h.shape
    proj = h @ p['w_in']
    offs = np.cumsum([NA_WIDTH, NA_WIDTH, NA_WIDTH, Q_LORA, KV_LORA]).tolist()
    q_na, k_na, v_na, cq, ckv, kr = jnp.split(proj, offs, axis=-1)
    shp = (B, S, NA_HEADS, HEAD_DIM)
    na_out = neighbourhood_attention(q_na.reshape(shp), k_na.reshape(shp), v_na.reshape(shp), p['na_rpb'])
    mla_out = latent_attention(cq, ckv, kr, p['q_a_norm'], p['w_q_b'], p['kv_a_norm'], p['w_kv_b'])
    merged = jnp.concatenate([rms_norm(na_out, p['g_na_out']), rms_norm(mla_out, p['g_mla_out'])], axis=-1)
    return merged @ p['w_o']


def moe_ffn(h, w_router, b_router, w1, b1, w2, b2):
    N, D = h.shape
    logits = (h @ w_router + b_router).astype(jnp.float32)
    top_v, top_e = lax.top_k(logits, TOP_K)
    gates = jax.nn.softmax(top_v, axis=-1)
    nk = N * TOP_K
    flat_e = top_e.reshape(-1).astype(jnp.int32)
    flat_tok = jnp.repeat(jnp.arange(N, dtype=jnp.int32), TOP_K)
    flat_g = gates.reshape(-1)
    order = jnp.argsort(flat_e, stable=True)
    e_sorted = flat_e[order]
    counts = jnp.bincount(flat_e, length=N_EXPERTS).astype(jnp.int32)
    padded = (counts + MOE_BLOCK - 1) // MOE_BLOCK * MOE_BLOCK
    start = jnp.cumsum(counts) - counts
    pad_end = jnp.cumsum(padded)
    pad_start = pad_end - padded
    dest = pad_start[e_sorted] + jnp.arange(nk, dtype=jnp.int32) - start[e_sorted]
    n_blocks = (nk + N_EXPERTS * (MOE_BLOCK - 1) + MOE_BLOCK - 1) // MOE_BLOCK
    n_slots = n_blocks * MOE_BLOCK
    slot_tok = jnp.full((n_slots,), N, dtype=jnp.int32).at[dest].set(flat_tok[order])
    slot_gate = jnp.zeros((n_slots,), jnp.float32).at[dest].set(flat_g[order])
    block_start = jnp.arange(n_blocks, dtype=jnp.int32) * MOE_BLOCK
    block_e = jnp.minimum(jnp.searchsorted(pad_end, block_start, side='right'), N_EXPERTS - 1)
    h_pad = jnp.concatenate([h, jnp.zeros((1, D), h.dtype)], axis=0)

    def one_block(args):
        tok, g, e = args
        xb = h_pad[tok]
        u = xb @ w1[e] + b1[e]
        x_glu = jnp.minimum(u[:, ::2], SWIGLU_LIMIT)
        x_lin = jnp.clip(u[:, 1::2], -SWIGLU_LIMIT, SWIGLU_LIMIT)
        act = (x_lin + 1.0) * (x_glu * jax.nn.sigmoid(x_glu * SWIGLU_ALPHA))
        y = act @ w2[e] + b2[e]
        return (y * g[:, None]).astype(h.dtype)

    yb = lax.map(one_block, (slot_tok.reshape(n_blocks, MOE_BLOCK),
                             slot_gate.reshape(n_blocks, MOE_BLOCK), block_e))
    out = jnp.zeros((N + 1, D), h.dtype).at[slot_tok].add(yb.reshape(n_slots, D))
    return out[:N]


def encoder_layer(x, c, p):
    B, S, D = x.shape
    ada = jax.nn.silu(c) @ p['w_ada'] + p['b_ada']
    sh_a, sc_a, g_a, sh_m, sc_m, g_m = [a[:, None, :] for a in jnp.split(ada, 6, axis=-1)]
    h = rms_norm(x, p['ln_pre_mix']) * (1.0 + sc_a) + sh_a
    x = x + g_a * rms_norm(token_mixer(h, p), p['ln_post_mix'])
    h = rms_norm(x, p['ln_pre_moe']) * (1.0 + sc_m) + sh_m
    y = moe_ffn(h.reshape(B * S, D), p['w_router'], p['b_router'], p['w_mlp1'], p['b_mlp1'],
                p['w_mlp2'], p['b_mlp2']).reshape(B, S, D)
    return x + g_m * rms_norm(y, p['ln_post_moe'])


def setup_inputs(seed: int = 0) -> dict:
    key = jax.random.key(seed)
    ks = jax.random.split(key, 32)

    def nrm(k, shape, scale):
        return jax.random.normal(k, shape, jnp.float32) * scale

    def gain(k, n):
        return 1.0 + nrm(k, (DEPTH, n), 0.1)

    D = D_MODEL
    return {
        'x_prompt': nrm(ks[0], (BATCH, SEQ, D), 1.0),
        'x_sample': nrm(ks[1], (DEC_BATCH, DEC_SEQ, D), 1.0),
        'c_prompt': nrm(ks[2], (BATCH, D), 1.0),
        'c_sample': nrm(ks[3], (DEC_BATCH, D), 1.0),
        'ln_pre_mix': gain(ks[4], D),
        'ln_post_mix': gain(ks[5], D),
        'ln_pre_moe': gain(ks[6], D),
        'ln_post_moe': gain(ks[7], D),
        'w_ada': nrm(ks[8], (DEPTH, D, 6 * D), 0.5 * D ** -0.5),
        'b_ada': nrm(ks[9], (DEPTH, 6 * D), 0.02),
        'w_in': nrm(ks[10], (DEPTH, D, IN_WIDTH), D ** -0.5),
        'na_rpb': nrm(ks[11], (DEPTH, NA_HEADS, 2 * NA_KR - 1, 2 * NA_KC - 1), 0.1),
        'q_a_norm': gain(ks[12], Q_LORA),
        'w_q_b': nrm(ks[13], (DEPTH, Q_LORA, MLA_HEADS * MLA_QK), Q_LORA ** -0.5),
        'kv_a_norm': gain(ks[14], KV_LORA),
        'w_kv_b': nrm(ks[15], (DEPTH, KV_LORA, MLA_HEADS * (MLA_NOPE + MLA_V)), KV_LORA ** -0.5),
        'g_na_out': gain(ks[16], NA_WIDTH),
        'g_mla_out': gain(ks[17], MLA_WIDTH),
        'w_o': nrm(ks[18], (DEPTH, MIX_WIDTH, D), MIX_WIDTH ** -0.5),
        'w_router': nrm(ks[19], (DEPTH, D, N_EXPERTS), D ** -0.5),
        'b_router': nrm(ks[20], (DEPTH, N_EXPERTS), 0.01),
        'w_mlp1': nrm(ks[21], (DEPTH, N_EXPERTS, D, 2 * D_FF), D ** -0.5),
        'b_mlp1': nrm(ks[22], (DEPTH, N_EXPERTS, 2 * D_FF), 0.02),
        'w_mlp2': nrm(ks[23], (DEPTH, N_EXPERTS, D_FF, D), D_FF ** -0.5),
        'b_mlp2': nrm(ks[24], (DEPTH, N_EXPERTS, D), 0.02),
    }


def reference(x_prompt, x_sample, c_prompt, c_sample, ln_pre_mix, ln_post_mix, ln_pre_moe, ln_post_moe,
              w_ada, b_ada, w_in, na_rpb, q_a_norm, w_q_b, kv_a_norm, w_kv_b, g_na_out, g_mla_out, w_o,
              w_router, b_router, w_mlp1, b_mlp1, w_mlp2, b_mlp2):
    y_prompt = x_prompt
    y_sample = x_sample
    for l in range(DEPTH):
        p = {
            'ln_pre_mix': ln_pre_mix[l], 'ln_post_mix': ln_post_mix[l],
            'ln_pre_moe': ln_pre_moe[l], 'ln_post_moe': ln_post_moe[l],
            'w_ada': w_ada[l], 'b_ada': b_ada[l], 'w_in': w_in[l], 'na_rpb': na_rpb[l],
            'q_a_norm': q_a_norm[l], 'w_q_b': w_q_b[l], 'kv_a_norm': kv_a_norm[l], 'w_kv_b': w_kv_b[l],
            'g_na_out': g_na_out[l], 'g_mla_out': g_mla_out[l], 'w_o': w_o[l],
            'w_router': w_router[l], 'b_router': b_router[l],
            'w_mlp1': w_mlp1[l], 'b_mlp1': b_mlp1[l], 'w_mlp2': w_mlp2[l], 'b_mlp2': b_mlp2[l],
        }
        y_prompt = encoder_layer(y_prompt, c_prompt, p)
        y_sample = encoder_layer(y_sample, c_sample, p)
    return (y_prompt, y_sample)
```

```python
import functools

import numpy as np
import jax
import jax.numpy as jnp
from jax import lax
from jax.experimental import pallas as pl
from jax.experimental.pallas import tpu as pltpu

F32 = jnp.float32
BF16 = jnp.bfloat16

GRID_W = 64
HEAD_DIM = 64
NA_HEADS = 8
NA_KR = 8
NA_KC = 16
MLA_HEADS = 8
MLA_NOPE = 64
MLA_ROPE = 32
MLA_V = 64
ROPE_THETA = 10000.0
N_EXPERTS = 32
TOP_K = 4
SWIGLU_LIMIT = 7.0
SWIGLU_ALPHA = 1.702
EPS = 1e-6

LANE = 128
HEAD_PAD = 128
NEG_BIG = -1e30
LOG2E = 1.4426950408889634

T_PROJ = 256
T_Q = 512
T_K = 512
NA_ROWS = 8
T_POST = 256
MOE_BLK = 512
T_COMB = 128
VMEM_LIMIT = 56 * 1024 * 1024


def _rms(x, n=None):
    n = x.shape[-1] if n is None else n
    return x * lax.rsqrt(jnp.sum(x * x, axis=-1, keepdims=True) * (1.0 / n) + EPS)


def _cparams(sem):
    return pltpu.CompilerParams(dimension_semantics=sem, vmem_limit_bytes=VMEM_LIMIT)


def _ada_kernel(c_ref, w_ref, b_ref, o_ref):
    c = c_ref[...]
    s = c / (1.0 + jnp.exp(-c))
    o_ref[...] = jnp.dot(s, w_ref[...], preferred_element_type=F32,
                         precision=lax.Precision.HIGHEST) + b_ref[...]


def _ada(c_all, w_ada, b_ada):
    rows, d = c_all.shape
    n_out = w_ada.shape[1]
    return pl.pallas_call(
        _ada_kernel,
        grid=(n_out // d,),
        in_specs=[pl.BlockSpec((rows, d), lambda j: (0, 0)),
                  pl.BlockSpec((d, d), lambda j: (0, j)),
                  pl.BlockSpec((1, d), lambda j: (0, j))],
        out_specs=pl.BlockSpec((rows, d), lambda j: (0, j)),
        out_shape=jax.ShapeDtypeStruct((rows, n_out), F32),
        compiler_params=_cparams(("arbitrary",)),
        name="ada",
    )(c_all, w_ada, b_ada.reshape(1, n_out))


def _inproj_kernel(x_ref, ada_ref, g_ref, ck_ref, sk_ref, w1_ref, qan_ref, wqa_ref, wqb_ref,
                   kvan_ref, wk_ref, wv_ref,
                   qna_ref, kna_ref, vna_ref, q_ref, k_ref, v_ref, *, na_w, q_lora, kv_lora, qscale):
    x = x_ref[0]
    sh = ada_ref[0, 0:1, :]
    sc = ada_ref[0, 1:2, :]
    h = (_rms(x) * g_ref[...]) * (1.0 + sc) + sh
    proj = jnp.dot(h.astype(BF16), w1_ref[...], preferred_element_type=F32)
    qna_ref[0] = (proj[:, 0:na_w] * (1.0 / float(np.sqrt(HEAD_DIM)))).astype(BF16)
    kna_ref[0] = proj[:, na_w:2 * na_w].astype(BF16)
    vna_ref[0] = proj[:, 2 * na_w:3 * na_w].astype(BF16)
    o = 3 * na_w
    cqn = (_rms(proj[:, o:o + q_lora]) * qan_ref[...]).astype(BF16)
    o += q_lora
    ckvn = (_rms(proj[:, o:o + kv_lora]) * kvan_ref[...]).astype(BF16)
    o += kv_lora
    kra = proj[:, o:o + HEAD_PAD]
    krb = proj[:, o + HEAD_PAD:o + 2 * HEAD_PAD]
    qa = jnp.dot(cqn, wqa_ref[...], preferred_element_type=F32)
    qb = jnp.dot(cqn, wqb_ref[...], preferred_element_type=F32)
    kn = jnp.dot(ckvn, wk_ref[...], preferred_element_type=F32)
    vv = jnp.dot(ckvn, wv_ref[...], preferred_element_type=F32)
    ck = ck_ref[...]
    sk = sk_ref[...]
    lane = lax.broadcasted_iota(jnp.int32, (1, HEAD_PAD), 1)
    nope = (lane < MLA_NOPE).astype(F32)
    ones_col = (lane == MLA_V).astype(F32)
    cq = (nope + ck) * qscale
    sq = sk * qscale
    kpe = kra * ck + krb * sk
    for hd in range(MLA_HEADS):
        sl = slice(hd * HEAD_PAD, (hd + 1) * HEAD_PAD)
        q_ref[0, :, sl] = (qa[:, sl] * cq + qb[:, sl] * sq).astype(BF16)
        k_ref[0, :, sl] = (kn[:, sl] + kpe).astype(BF16)
        v_ref[0, :, sl] = (vv[:, sl] + ones_col).astype(BF16)


def _inproj(x, ada3, p, ck, sk):
    b, s, d = x.shape
    t = min(T_PROJ, s)
    na_w = NA_HEADS * HEAD_DIM
    hw = MLA_HEADS * HEAD_PAD
    q_lora = p['wqa'].shape[0]
    kv_lora = p['wk'].shape[0]
    w1 = p['w1']
    kern = functools.partial(_inproj_kernel, na_w=na_w, q_lora=q_lora, kv_lora=kv_lora,
                             qscale=LOG2E / float(np.sqrt(MLA_NOPE + MLA_ROPE)))
    full = lambda a: pl.BlockSpec(a.shape, lambda i, j: (0,) * a.ndim)
    tok = lambda w: pl.BlockSpec((1, t, w), lambda i, j: (i, j, 0))
    outs = pl.pallas_call(
        kern,
        grid=(b, s // t),
        in_specs=[tok(d),
                  pl.BlockSpec((1, 6, d), lambda i, j: (i, 0, 0)),
                  full(p['ln_pre_mix']),
                  pl.BlockSpec((t, HEAD_PAD), lambda i, j: (j, 0)),
                  pl.BlockSpec((t, HEAD_PAD), lambda i, j: (j, 0)),
                  full(w1), full(p['q_a_norm']), full(p['wqa']), full(p['wqb']),
                  full(p['kv_a_norm']), full(p['wk']), full(p['wv'])],
        out_specs=[tok(na_w), tok(na_w), tok(na_w), tok(hw), tok(hw), tok(hw)],
        out_shape=[jax.ShapeDtypeStruct((b, s, na_w), BF16)] * 3
                  + [jax.ShapeDtypeStruct((b, s, hw), BF16)] * 3,
        compiler_params=_cparams(("arbitrary", "arbitrary")),
        name="inproj",
    )(x, ada3, p['ln_pre_mix'], ck, sk, w1, p['q_a_norm'], p['wqa'], p['wqb'],
      p['kv_a_norm'], p['wk'], p['wv'])
    return outs


def _na_kernel(q_ref, kp_ref, kc_ref, kn_ref, vp_ref, vc_ref, vn_ref, bias_ref, o_ref,
               kcat, vcat, *, rows):
    blk = pl.program_id(1)
    nb = NA_ROWS * GRID_W
    win = NA_KR * GRID_W
    for i, (kr, vr) in enumerate(((kp_ref, vp_ref), (kc_ref, vc_ref), (kn_ref, vn_ref))):
        kcat[i * nb:(i + 1) * nb, :] = kr[0]
        vcat[i * nb:(i + 1) * nb, :] = vr[0]
    lane = lax.broadcasted_iota(jnp.int32, (GRID_W, LANE), 1)
    lo = lane < HEAD_DIM

    def row_body(rl, carry):
        r = blk * NA_ROWS + rl
        rs = jnp.clip(r - NA_KR // 2, 0, rows - NA_KR)
        d = r - rs
        off = pl.multiple_of((rs - (blk - 1) * NA_ROWS) * GRID_W, GRID_W)
        qoff = pl.multiple_of(rl * GRID_W, GRID_W)
        for hp in range(NA_HEADS // 2):
            sl = slice(hp * LANE, (hp + 1) * LANE)
            qp = q_ref[0, pl.ds(qoff, GRID_W), sl]
            kw = kcat[pl.ds(off, win), sl]
            vw = vcat[pl.ds(off, win), sl]
            outs = []
            for hh in range(2):
                qm = jnp.where(lo if hh == 0 else jnp.logical_not(lo), qp, jnp.zeros_like(qp))
                s = lax.dot_general(qm, kw, (((1,), (1,)), ((), ())), preferred_element_type=F32)
                s = s + bias_ref[d, hp * 2 + hh]
                m = jnp.max(s, axis=-1, keepdims=True)
                e = jnp.exp(s - m)
                l = jnp.sum(e, axis=-1, keepdims=True)
                pv = jnp.dot(e.astype(BF16), vw, preferred_element_type=F32)
                outs.append(pv / l)
            o_ref[0, pl.ds(qoff, GRID_W), sl] = jnp.where(lo, outs[0], outs[1])
        return carry

    lax.fori_loop(0, NA_ROWS, row_body, 0)


def _na_bias_table(rpb):
    cols = np.arange(GRID_W)
    cstart = np.clip(cols - NA_KC // 2, 0, GRID_W - NA_KC)
    j = np.arange(GRID_W)
    inwin = (j[None, :] >= cstart[:, None]) & (j[None, :] < cstart[:, None] + NA_KC)
    colrel = np.clip(j[None, :] - cols[:, None] + NA_KC - 1, 0, 2 * NA_KC - 2)
    dd = np.arange(NA_KR)
    ii = np.arange(NA_KR)
    rowrel = ii[None, :] - dd[:, None] + NA_KR - 1
    tab = rpb[:, rowrel[:, None, :, None], colrel[None, :, None, :]]
    tab = jnp.where(jnp.asarray(inwin)[None, None, :, None, :], tab.astype(F32), NEG_BIG)
    return tab.transpose(1, 0, 2, 3, 4).reshape(NA_KR, rpb.shape[0], GRID_W, NA_KR * GRID_W)


def _na(q, k, v, bias):
    b, s, w = q.shape
    rows = s // GRID_W
    assert rows % NA_ROWS == 0 and rows >= 2 * NA_ROWS
    nblk = rows // NA_ROWS
    nb = NA_ROWS * GRID_W
    cur = pl.BlockSpec((1, nb, w), lambda i, j: (i, j, 0))
    prv = pl.BlockSpec((1, nb, w), lambda i, j: (i, jnp.maximum(j - 1, 0), 0))
    nxt = pl.BlockSpec((1, nb, w), lambda i, j: (i, jnp.minimum(j + 1, nblk - 1), 0))
    return pl.pallas_call(
        functools.partial(_na_kernel, rows=rows),
        grid=(b, nblk),
        in_specs=[cur, prv, cur, nxt, prv, cur, nxt,
                  pl.BlockSpec(bias.shape, lambda i, j: (0, 0, 0, 0))],
        out_specs=cur,
        out_shape=jax.ShapeDtypeStruct((b, s, w), F32),
        scratch_shapes=[pltpu.VMEM((3 * nb, w), BF16), pltpu.VMEM((3 * nb, w), BF16)],
        compiler_params=_cparams(("arbitrary", "arbitrary")),
        name="na",
    )(q, k, k, k, v, v, v, bias)


def _flash_kernel(q_ref, k_ref, v_ref, o_ref, m_sc, acc_sc, *, tk, nk):
    q = q_ref[0]
    m_sc[...] = jnp.full(m_sc.shape, -jnp.inf, F32)
    acc_sc[...] = jnp.zeros(acc_sc.shape, F32)

    def body(j, carry):
        off = pl.multiple_of(j * tk, tk)
        ks = k_ref[0, pl.ds(off, tk), :]
        vs = v_ref[0, pl.ds(off, tk), :]
        s = lax.dot_general(q, ks, (((1,), (1,)), ((), ())), preferred_element_type=F32)
        m_prev = m_sc[...]
        m_new = jnp.maximum(m_prev, jnp.max(s, axis=-1, keepdims=True))
        alpha = jnp.exp2(m_prev - m_new)
        p = jnp.exp2(s - m_new[:, 0:1])
        acc_sc[...] = alpha * acc_sc[...] + jnp.dot(p.astype(BF16), vs, preferred_element_type=F32)
        m_sc[...] = m_new
        return carry

    lax.fori_loop(0, nk, body, 0)
    acc = acc_sc[...]
    o_ref[0] = acc / acc[:, MLA_V:MLA_V + 1]


def _flash(q, k, v):
    b, s, hw = q.shape
    nh = hw // HEAD_PAD
    tq = min(T_Q, s)
    tk = min(T_K, s)
    return pl.pallas_call(
        functools.partial(_flash_kernel, tk=tk, nk=s // tk),
        grid=(b, nh, s // tq),
        in_specs=[pl.BlockSpec((1, tq, HEAD_PAD), lambda i, h, j: (i, j, h)),
                  pl.BlockSpec((1, s, HEAD_PAD), lambda i, h, j: (i, 0, h)),
                  pl.BlockSpec((1, s, HEAD_PAD), lambda i, h, j: (i, 0, h))],
        out_specs=pl.BlockSpec((1, tq, HEAD_PAD), lambda i, h, j: (i, j, h)),
        out_shape=jax.ShapeDtypeStruct((b, s, hw), F32),
        scratch_shapes=[pltpu.VMEM((tq, HEAD_PAD), F32), pltpu.VMEM((tq, HEAD_PAD), F32)],
        compiler_params=_cparams(("arbitrary", "arbitrary", "arbitrary")),
        name="flash",
    )(q, k, v)


def _post_kernel(na_ref, mla_ref, x_ref, ada_ref, gna_ref, gmla_ref, wona_ref, womla_ref,
                 lnpost_ref, lnpre_ref, wr_ref, br_ref,
                 x1_ref, h2_ref, e_ref, g_ref, rank_ref, cnt_ref, base_sc):
    first = jnp.logical_and(pl.program_id(0) == 0, pl.program_id(1) == 0)

    @pl.when(first)
    def _():
        base_sc[...] = jnp.zeros(base_sc.shape, F32)

    t = x_ref.shape[1]
    na = na_ref[0]
    nan_ = _rms(na) * gna_ref[...]
    ml = mla_ref[0]
    lane_w = lax.broadcasted_iota(jnp.int32, ml.shape, 1)
    ml = jnp.where((lane_w % HEAD_PAD) < MLA_V, ml, 0.0)
    mln = _rms(ml, n=MLA_HEADS * MLA_V) * gmla_ref[...]
    mix = (jnp.dot(nan_.astype(BF16), wona_ref[...], preferred_element_type=F32)
           + jnp.dot(mln.astype(BF16), womla_ref[...], preferred_element_type=F32))
    g_a = ada_ref[0, 2:3, :]
    sh_m = ada_ref[0, 3:4, :]
    sc_m = ada_ref[0, 4:5, :]
    x1 = x_ref[0] + g_a * (_rms(mix) * lnpost_ref[...])
    x1_ref[0] = x1
    h2 = (_rms(x1) * lnpre_ref[...]) * (1.0 + sc_m) + sh_m
    h2_ref[0] = h2

    logits = jnp.dot(h2, wr_ref[...], preferred_element_type=F32,
                     precision=lax.Precision.HIGHEST) + br_ref[...]
    lane = lax.broadcasted_iota(jnp.int32, (t, LANE), 1)
    lane_f = lane.astype(F32)
    work = jnp.where(lane < N_EXPERTS, logits, -jnp.inf)
    vals, idxs = [], []
    onehot = jnp.zeros((t, LANE), F32)
    for _ in range(TOP_K):
        mk = jnp.max(work, axis=-1, keepdims=True)
        ik = jnp.min(jnp.where(work == mk, lane_f, float(LANE)), axis=-1, keepdims=True)
        sel = lane_f == ik
        work = jnp.where(sel, -jnp.inf, work)
        onehot = jnp.where(sel, 1.0, onehot)
        vals.append(mk)
        idxs.append(ik)
    exps = [jnp.exp(vk - vals[0]) for vk in vals]
    denom = exps[0]
    for ek in exps[1:]:
        denom = denom + ek

    r_i = lax.broadcasted_iota(jnp.int32, (t, t), 0)
    c_i = lax.broadcasted_iota(jnp.int32, (t, t), 1)
    tril = jnp.where(c_i < r_i, 1.0, 0.0).astype(BF16)
    pref = jnp.dot(tril, onehot.astype(BF16), preferred_element_type=F32) + base_sc[...]

    e_out = jnp.zeros((t, LANE), F32)
    g_out = jnp.zeros((t, LANE), F32)
    r_out = jnp.zeros((t, LANE), F32)
    for kk in range(TOP_K):
        rk = jnp.sum(jnp.where(lane_f == idxs[kk], pref, 0.0), axis=-1, keepdims=True)
        here = lane == kk
        e_out = jnp.where(here, idxs[kk], e_out)
        g_out = jnp.where(here, exps[kk] / denom, g_out)
        r_out = jnp.where(here, rk, r_out)
    e_ref[0] = e_out.astype(jnp.int32)
    g_ref[0] = g_out
    rank_ref[0] = r_out.astype(jnp.int32)
    base_sc[...] = base_sc[...] + jnp.sum(onehot, axis=0, keepdims=True)
    cnt_ref[...] = jnp.broadcast_to(base_sc[...], cnt_ref.shape)


def _post(na_out, mla_out, x, ada3, p):
    b, s, d = x.shape
    t = min(T_POST, s)
    full = lambda a: pl.BlockSpec(a.shape, lambda i, j: (0,) * a.ndim)
    tok = lambda w: pl.BlockSpec((1, t, w), lambda i, j: (i, j, 0))
    return pl.pallas_call(
        _post_kernel,
        grid=(b, s // t),
        in_specs=[tok(na_out.shape[-1]), tok(mla_out.shape[-1]), tok(d),
                  pl.BlockSpec((1, 6, d), lambda i, j: (i, 0, 0)),
                  full(p['g_na_out']), full(p['g_mla_pad']), full(p['wo_na']), full(p['wo_mla']),
                  full(p['ln_post_mix']), full(p['ln_pre_moe']), full(p['w_router']), full(p['b_router'])],
        out_specs=[tok(d), tok(d), tok(LANE), tok(LANE), tok(LANE),
                   pl.BlockSpec((8, LANE), lambda i, j: (0, 0))],
        out_shape=[jax.ShapeDtypeStruct((b, s, d), F32), jax.ShapeDtypeStruct((b, s, d), F32),
                   jax.ShapeDtypeStruct((b, s, LANE), jnp.int32), jax.ShapeDtypeStruct((b, s, LANE), F32),
                   jax.ShapeDtypeStruct((b, s, LANE), jnp.int32), jax.ShapeDtypeStruct((8, LANE), F32)],
        scratch_shapes=[pltpu.VMEM((1, LANE), F32)],
        compiler_params=_cparams(("arbitrary", "arbitrary")),
        name="post",
    )(na_out, mla_out, x, ada3, p['g_na_out'], p['g_mla_pad'], p['wo_na'], p['wo_mla'],
      p['ln_post_mix'], p['ln_pre_moe'], p['w_router'], p['b_router'])


def _experts_kernel(be_ref, nact_ref, tok_hbm, h_hbm, w1g_ref, w1l_ref, w2_ref, b1g_ref, b1l_ref, b2_ref,
                    y_ref, idx_sm, xbuf, isem, rsem, *, blk):
    i = pl.program_id(0)
    nact = nact_ref[0]

    def idx_copy(step, slot):
        return pltpu.make_async_copy(tok_hbm.at[step], idx_sm.at[slot], isem.at[slot])

    def issue_rows(slot):
        def one(j, carry):
            tk = idx_sm[slot, j]
            pltpu.make_async_copy(h_hbm.at[pl.ds(tk, 1), :], xbuf.at[slot, pl.ds(j, 1), :],
                                  rsem.at[slot]).start()
            return carry
        lax.fori_loop(0, blk, one, 0)

    def wait_rows(slot):
        pltpu.make_async_copy(h_hbm.at[pl.ds(0, blk), :], xbuf.at[slot], rsem.at[slot]).wait()

    @pl.when(jnp.logical_and(i == 0, nact > 0))
    def _():
        idx_copy(0, 0).start()
        idx_copy(0, 0).wait()
        issue_rows(0)

        @pl.when(nact > 1)
        def _():
            idx_copy(1, 1).start()

    cur = lax.rem(i, 2)
    nxt = 1 - cur

    @pl.when(i + 1 < nact)
    def _():
        idx_copy(i + 1, nxt).wait()
        issue_rows(nxt)

        @pl.when(i + 2 < nact)
        def _():
            idx_copy(i + 2, cur).start()

    @pl.when(i < nact)
    def _():
        wait_rows(cur)
        x = xbuf[cur].astype(BF16)
        ug = jnp.dot(x, w1g_ref[0], preferred_element_type=F32) + b1g_ref[0]
        ul = jnp.dot(x, w1l_ref[0], preferred_element_type=F32) + b1l_ref[0]
        x_glu = jnp.minimum(ug, SWIGLU_LIMIT)
        x_lin = jnp.clip(ul, -SWIGLU_LIMIT, SWIGLU_LIMIT)
        act = (x_lin + 1.0) * (x_glu * (1.0 / (1.0 + jnp.exp(-SWIGLU_ALPHA * x_glu))))
        y_ref[...] = jnp.dot(act.astype(BF16), w2_ref[0], preferred_element_type=F32) + b2_ref[0]

    @pl.when(i >= nact)
    def _():
        y_ref[...] = jnp.zeros(y_ref.shape, F32)


def _experts(h2, slot_tok, block_e, nact, p):
    n, d = h2.shape
    nb, blk = slot_tok.shape
    f = p['w1g'].shape[2]
    wspec = lambda a: pl.BlockSpec((1,) + a.shape[1:], lambda i, be, na: (be[i], 0, 0))
    grid_spec = pltpu.PrefetchScalarGridSpec(
        num_scalar_prefetch=2,
        grid=(nb,),
        in_specs=[pl.BlockSpec(memory_space=pl.ANY), pl.BlockSpec(memory_space=pl.ANY),
                  wspec(p['w1g']), wspec(p['w1l']), wspec(p['w2']),
                  wspec(p['b1g']), wspec(p['b1l']), wspec(p['b2'])],
        out_specs=pl.BlockSpec((blk, d), lambda i, be, na: (i, 0)),
        scratch_shapes=[pltpu.SMEM((2, blk), jnp.int32), pltpu.VMEM((2, blk, d), F32),
                        pltpu.SemaphoreType.DMA((2,)), pltpu.SemaphoreType.DMA((2,))],
    )
    return pl.pallas_call(
        functools.partial(_experts_kernel, blk=blk),
        grid_spec=grid_spec,
        out_shape=jax.ShapeDtypeStruct((nb * blk, d), F32),
        compiler_params=_cparams(("arbitrary",)),
        name="experts",
    )(block_e, nact, slot_tok, h2, p['w1g'], p['w1l'], p['w2'], p['b1g'], p['b1l'], p['b2'])


def _combine_kernel(dest_hbm, y_hbm, x1_ref, g_ref, ada_ref, ln_ref, o_ref,
                    idx_sm, ybuf, isem, rsem, *, t, nsteps):
    i = pl.program_id(0) * pl.num_programs(1) + pl.program_id(1)

    def idx_copy(step, slot):
        return pltpu.make_async_copy(dest_hbm.at[step], idx_sm.at[slot], isem.at[slot])

    def issue_rows(slot):
        def one(j, carry):
            for kk in range(TOP_K):
                ds_ = idx_sm[slot, j * TOP_K + kk]
                pltpu.make_async_copy(y_hbm.at[pl.ds(ds_, 1), :], ybuf.at[slot, kk, pl.ds(j, 1), :],
                                      rsem.at[slot]).start()
            return carry
        lax.fori_loop(0, t, one, 0)

    def wait_rows(slot):
        for kk in range(TOP_K):
            pltpu.make_async_copy(y_hbm.at[pl.ds(0, t), :], ybuf.at[slot, kk], rsem.at[slot]).wait()

    @pl.when(i == 0)
    def _():
        idx_copy(0, 0).start()
        idx_copy(0, 0).wait()
        issue_rows(0)
        if nsteps > 1:
            idx_copy(1, 1).start()

    cur = lax.rem(i, 2)
    nxt = 1 - cur

    @pl.when(i + 1 < nsteps)
    def _():
        idx_copy(i + 1, nxt).wait()
        issue_rows(nxt)

        @pl.when(i + 2 < nsteps)
        def _():
            idx_copy(i + 2, cur).start()

    wait_rows(cur)
    g = g_ref[0]
    y = jnp.zeros((t, x1_ref.shape[2]), F32)
    for kk in range(TOP_K):
        y = y + ybuf[cur, kk] * g[:, kk:kk + 1]
    g_m = ada_ref[0, 5:6, :]
    o_ref[0] = x1_ref[0] + g_m * (_rms(y) * ln_ref[...])


def _combine(dest, y_sorted, x1, gates, ada3, ln_post_moe):
    b, s, d = x1.shape
    t = min(T_COMB, s)
    ns = s // t
    nsteps = b * ns
    dest2 = dest.reshape(nsteps, t * TOP_K)
    return pl.pallas_call(
        functools.partial(_combine_kernel, t=t, nsteps=nsteps),
        grid=(b, ns),
        in_specs=[pl.BlockSpec(memory_space=pl.ANY), pl.BlockSpec(memory_space=pl.ANY),
                  pl.BlockSpec((1, t, d), lambda i, j: (i, j, 0)),
                  pl.BlockSpec((1, t, LANE), lambda i, j: (i, j, 0)),
                  pl.BlockSpec((1, 6, d), lambda i, j: (i, 0, 0)),
                  pl.BlockSpec(ln_post_moe.shape, lambda i, j: (0, 0))],
        out_specs=pl.BlockSpec((1, t, d), lambda i, j: (i, j, 0)),
        out_shape=jax.ShapeDtypeStruct((b, s, d), F32),
        scratch_shapes=[pltpu.SMEM((2, t * TOP_K), jnp.int32), pltpu.VMEM((2, TOP_K, t, d), F32),
                        pltpu.SemaphoreType.DMA((2,)), pltpu.SemaphoreType.DMA((2,))],
        compiler_params=_cparams(("arbitrary", "arbitrary")),
        name="combine",
    )(dest2, y_sorted, x1, gates, ada3, ln_post_moe)


def _rope_partner(w):
    half = MLA_ROPE // 2
    return jnp.concatenate([-w[..., half:], w[..., :half]], axis=-1)


def _prep_params(ln_pre_mix, ln_post_mix, ln_pre_moe, ln_post_moe, w_in, na_rpb, q_a_norm, w_q_b,
                 kv_a_norm, w_kv_b, g_na_out, g_mla_out, w_o, w_router, b_router,
                 w_mlp1, b_mlp1, w_mlp2, b_mlp2):
    d = w_in.shape[0]
    na_w = NA_HEADS * HEAD_DIM
    q_lora = w_q_b.shape[0]
    kv_lora = w_kv_b.shape[0]
    qk = MLA_NOPE + MLA_ROPE
    main_w = 3 * na_w + q_lora + kv_lora
    w_kr = w_in[:, main_w:main_w + MLA_ROPE]
    zpad = lambda n: jnp.zeros((d, n), F32)
    kra = jnp.concatenate([zpad(MLA_NOPE), w_kr, zpad(HEAD_PAD - qk)], axis=1)
    krb = jnp.concatenate([zpad(MLA_NOPE), _rope_partner(w_kr), zpad(HEAD_PAD - qk)], axis=1)
    w1 = jnp.concatenate([w_in[:, :main_w], kra, krb], axis=1).astype(BF16)

    wq = w_q_b.reshape(q_lora, MLA_HEADS, qk)
    zq = lambda n: jnp.zeros((q_lora, MLA_HEADS, n), F32)
    wqa = jnp.concatenate([wq, zq(HEAD_PAD - qk)], axis=2).reshape(q_lora, -1).astype(BF16)
    wqb = jnp.concatenate([zq(MLA_NOPE), _rope_partner(wq[:, :, MLA_NOPE:]), zq(HEAD_PAD - qk)],
                          axis=2).reshape(q_lora, -1).astype(BF16)
    wkv = w_kv_b.reshape(kv_lora, MLA_HEADS, MLA_NOPE + MLA_V)
    zk = lambda n: jnp.zeros((kv_lora, MLA_HEADS, n), F32)
    wk = jnp.concatenate([wkv[:, :, :MLA_NOPE], zk(HEAD_PAD - MLA_NOPE)], axis=2).reshape(kv_lora, -1).astype(BF16)
    wv = jnp.concatenate([wkv[:, :, MLA_NOPE:], zk(HEAD_PAD - MLA_V)], axis=2).reshape(kv_lora, -1).astype(BF16)

    mla_w = MLA_HEADS * MLA_V
    g_mla_pad = jnp.concatenate([g_mla_out.reshape(MLA_HEADS, MLA_V),
                                 jnp.zeros((MLA_HEADS, HEAD_PAD - MLA_V), F32)], axis=1).reshape(1, -1)
    wo_mla = w_o[na_w:na_w + mla_w].reshape(MLA_HEADS, MLA_V, d)
    wo_mla = jnp.concatenate([wo_mla, jnp.zeros((MLA_HEADS, HEAD_PAD - MLA_V, d), F32)],
                             axis=1).reshape(MLA_HEADS * HEAD_PAD, d).astype(BF16)
    w_r = jnp.concatenate([w_router, jnp.zeros((d, LANE - N_EXPERTS), F32)], axis=1)
    b_r = jnp.concatenate([b_router, jnp.zeros((LANE - N_EXPERTS,), F32)]).reshape(1, LANE)
    ne = w_mlp1.shape[0]
    return {
        'ln_pre_mix': ln_pre_mix.reshape(1, d), 'ln_post_mix': ln_post_mix.reshape(1, d),
        'ln_pre_moe': ln_pre_moe.reshape(1, d), 'ln_post_moe': ln_post_moe.reshape(1, d),
        'w1': w1, 'q_a_norm': q_a_norm.reshape(1, q_lora), 'wqa': wqa, 'wqb': wqb,
        'kv_a_norm': kv_a_norm.reshape(1, kv_lora), 'wk': wk, 'wv': wv,
        'na_bias': _na_bias_table(na_rpb),
        'g_na_out': g_na_out.reshape(1, na_w), 'g_mla_pad': g_mla_pad,
        'wo_na': w_o[:na_w].astype(BF16), 'wo_mla': wo_mla,
        'w_router': w_r, 'b_router': b_r,
        'w1g': w_mlp1[:, :, 0::2].astype(BF16), 'w1l': w_mlp1[:, :, 1::2].astype(BF16),
        'w2': w_mlp2.astype(BF16),
        'b1g': b_mlp1[:, 0::2].reshape(ne, 1, -1), 'b1l': b_mlp1[:, 1::2].reshape(ne, 1, -1),
        'b2': b_mlp2.reshape(ne, 1, -1),
    }


def _rope_lane_tables(s):
    half = MLA_ROPE // 2
    inv = ROPE_THETA ** (-jnp.arange(half, dtype=F32) / half)
    ang = jnp.arange(s, dtype=F32)[:, None] * inv[None, :]
    cos, sin = jnp.cos(ang), jnp.sin(ang)
    z = lambda n: jnp.zeros((s, n), F32)
    tail = HEAD_PAD - MLA_NOPE - MLA_ROPE
    ck = jnp.concatenate([z(MLA_NOPE), cos, cos, z(tail)], axis=1)
    sk = jnp.concatenate([z(MLA_NOPE), sin, sin, z(tail)], axis=1)
    return ck, sk


def _moe_plan(e4, rank4, counts, n):
    nk = n * TOP_K
    padded = (counts + MOE_BLK - 1) // MOE_BLK * MOE_BLK
    pad_end = jnp.cumsum(padded)
    pad_start = pad_end - padded
    dest = pad_start[e4] + rank4
    nb = (nk + N_EXPERTS * (MOE_BLK - 1) + MOE_BLK - 1) // MOE_BLK
    tok = jnp.broadcast_to(jnp.arange(n, dtype=jnp.int32)[:, None], (n, TOP_K))
    slot_tok = jnp.zeros((nb * MOE_BLK,), jnp.int32).at[dest.reshape(-1)].set(tok.reshape(-1))
    block_start = jnp.arange(nb, dtype=jnp.int32) * MOE_BLK
    block_e = jnp.minimum(jnp.searchsorted(pad_end, block_start, side='right'),
                          N_EXPERTS - 1).astype(jnp.int32)
    nact = (pad_end[-1] // MOE_BLK).astype(jnp.int32).reshape(1)
    return dest.astype(jnp.int32), slot_tok.reshape(nb, MOE_BLK), block_e, nact


def _encoder_layer(x, ada3, p):
    b, s, d = x.shape
    ck, sk = _rope_lane_tables(s)
    qna, kna, vna, q, k, v = _inproj(x, ada3, p, ck, sk)
    na_out = _na(qna, kna, vna, p['na_bias'])
    mla_out = _flash(q, k, v)
    x1, h2, e, g, rank, cnt = _post(na_out, mla_out, x, ada3, p)
    n = b * s
    e4 = e.reshape(n, LANE)[:, :TOP_K]
    rank4 = rank.reshape(n, LANE)[:, :TOP_K]
    counts = cnt[0, :N_EXPERTS].astype(jnp.int32)
    dest, slot_tok, block_e, nact = _moe_plan(e4, rank4, counts, n)
    y_sorted = _experts(h2.reshape(n, d), slot_tok, block_e, nact, p)
    return _combine(dest, y_sorted, x1, g, ada3, p['ln_post_moe'])


def kernel(x_prompt, x_sample, c_prompt, c_sample, ln_pre_mix, ln_post_mix, ln_pre_moe, ln_post_moe, w_ada, b_ada, w_in, na_rpb, q_a_norm, w_q_b, kv_a_norm, w_kv_b, g_na_out, g_mla_out, w_o, w_router, b_router, w_mlp1, b_mlp1, w_mlp2, b_mlp2):
    depth = w_ada.shape[0]
    d = x_prompt.shape[-1]
    bp, bs = x_prompt.shape[0], x_sample.shape[0]
    rows = -(-(bp + bs) // 8) * 8
    y_prompt, y_sample = x_prompt, x_sample
    for l in range(depth):
        p = _prep_params(ln_pre_mix[l], ln_post_mix[l], ln_pre_moe[l], ln_post_moe[l], w_in[l], na_rpb[l],
                         q_a_norm[l], w_q_b[l], kv_a_norm[l], w_kv_b[l], g_na_out[l], g_mla_out[l], w_o[l],
                         w_router[l], b_router[l], w_mlp1[l], b_mlp1[l], w_mlp2[l], b_mlp2[l])
        c_all = jnp.concatenate([c_prompt, c_sample, jnp.zeros((rows - bp - bs, d), F32)], axis=0)
        ada3 = _ada(c_all, w_ada[l], b_ada[l]).reshape(rows, 6, d)
        y_prompt = _encoder_layer(y_prompt, ada3[:bp], p)
        y_sample = _encoder_layer(y_sample, ada3[bp:bp + bs], p)
    return (y_prompt, y_sample)
```

```python
import functools

import numpy as np
import jax
import jax.numpy as jnp
from jax import lax
from jax.experimental import pallas as pl
from jax.experimental.pallas import tpu as pltpu

F32 = jnp.float32
BF16 = jnp.bfloat16

GRID_W = 64
HEAD_DIM = 64
NA_HEADS = 8
NA_KR = 8
NA_KC = 16
MLA_HEADS = 8
MLA_NOPE = 64
MLA_ROPE = 32
MLA_V = 64
ROPE_THETA = 10000.0
N_EXPERTS = 32
TOP_K = 4
SWIGLU_LIMIT = 7.0
SWIGLU_ALPHA = 1.702
EPS = 1e-6

LANE = 128
HEAD_PAD = 128
NEG_BIG = -1e30
LOG2E = 1.4426950408889634

T_PROJ = 256
T_Q = 512
T_K = 512
NA_ROWS = 8
T_POST = 256
MOE_BLK = 512
T_COMB = 128
VMEM_LIMIT = 56 * 1024 * 1024


def _rms(x, n=None):
    n = x.shape[-1] if n is None else n
    return x * lax.rsqrt(jnp.sum(x * x, axis=-1, keepdims=True) * (1.0 / n) + EPS)


def _cparams(sem):
    return pltpu.CompilerParams(dimension_semantics=sem, vmem_limit_bytes=VMEM_LIMIT)


def _ada_kernel(c_ref, w_ref, b_ref, o_ref):
    c = c_ref[...]
    s = c / (1.0 + jnp.exp(-c))
    o_ref[...] = jnp.dot(s, w_ref[...], preferred_element_type=F32,
                         precision=lax.Precision.HIGHEST) + b_ref[...]


def _ada(c_all, w_ada, b_ada):
    rows, d = c_all.shape
    n_out = w_ada.shape[1]
    return pl.pallas_call(
        _ada_kernel,
        grid=(n_out // d,),
        in_specs=[pl.BlockSpec((rows, d), lambda j: (0, 0)),
                  pl.BlockSpec((d, d), lambda j: (0, j)),
                  pl.BlockSpec((1, d), lambda j: (0, j))],
        out_specs=pl.BlockSpec((rows, d), lambda j: (0, j)),
        out_shape=jax.ShapeDtypeStruct((rows, n_out), F32),
        compiler_params=_cparams(("arbitrary",)),
        name="ada",
    )(c_all, w_ada, b_ada.reshape(1, n_out))


def _inproj_kernel(x_ref, ada_ref, g_ref, ck_ref, sk_ref, w1_ref, qan_ref, wqa_ref, wqb_ref,
                   kvan_ref, wk_ref, wv_ref,
                   qna_ref, kna_ref, vna_ref, q_ref, k_ref, v_ref, *, na_w, q_lora, kv_lora, qscale):
    x = x_ref[0]
    sh = ada_ref[0, 0:1, :]
    sc = ada_ref[0, 1:2, :]
    h = (_rms(x) * g_ref[...]) * (1.0 + sc) + sh
    proj = jnp.dot(h.astype(BF16), w1_ref[...], preferred_element_type=F32)
    qna_ref[0] = (proj[:, 0:na_w] * (1.0 / float(np.sqrt(HEAD_DIM)))).astype(BF16)
    kna_ref[0] = proj[:, na_w:2 * na_w].astype(BF16)
    vna_ref[0] = proj[:, 2 * na_w:3 * na_w].astype(BF16)
    o = 3 * na_w
    cqn = (_rms(proj[:, o:o + q_lora]) * qan_ref[...]).astype(BF16)
    o += q_lora
    ckvn = (_rms(proj[:, o:o + kv_lora]) * kvan_ref[...]).astype(BF16)
    o += kv_lora
    kra = proj[:, o:o + HEAD_PAD]
    krb = proj[:, o + HEAD_PAD:o + 2 * HEAD_PAD]
    qa = jnp.dot(cqn, wqa_ref[...], preferred_element_type=F32)
    qb = jnp.dot(cqn, wqb_ref[...], preferred_element_type=F32)
    kn = jnp.dot(ckvn, wk_ref[...], preferred_element_type=F32)
    vv = jnp.dot(ckvn, wv_ref[...], preferred_element_type=F32)
    ck = ck_ref[...]
    sk = sk_ref[...]
    lane = lax.broadcasted_iota(jnp.int32, (1, HEAD_PAD), 1)
    nope = (lane < MLA_NOPE).astype(F32)
    ones_col = (lane == MLA_V).astype(F32)
    cq = (nope + ck) * qscale
    sq = sk * qscale
    kpe = kra * ck + krb * sk
    for hd in range(MLA_HEADS):
        sl = slice(hd * HEAD_PAD, (hd + 1) * HEAD_PAD)
        q_ref[0, :, sl] = (qa[:, sl] * cq + qb[:, sl] * sq).astype(BF16)
        k_ref[0, :, sl] = (kn[:, sl] + kpe).astype(BF16)
        v_ref[0, :, sl] = (vv[:, sl] + ones_col).astype(BF16)


def _inproj(x, ada3, p, ck, sk):
    b, s, d = x.shape
    t = min(T_PROJ, s)
    na_w = NA_HEADS * HEAD_DIM
    hw = MLA_HEADS * HEAD_PAD
    q_lora = p['wqa'].shape[0]
    kv_lora = p['wk'].shape[0]
    w1 = p['w1']
    kern = functools.partial(_inproj_kernel, na_w=na_w, q_lora=q_lora, kv_lora=kv_lora,
                             qscale=LOG2E / float(np.sqrt(MLA_NOPE + MLA_ROPE)))
    full = lambda a: pl.BlockSpec(a.shape, lambda i, j: (0,) * a.ndim)
    tok = lambda w: pl.BlockSpec((1, t, w), lambda i, j: (i, j, 0))
    outs = pl.pallas_call(
        kern,
        grid=(b, s // t),
        in_specs=[tok(d),
                  pl.BlockSpec((1, 6, d), lambda i, j: (i, 0, 0)),
                  full(p['ln_pre_mix']),
                  pl.BlockSpec((t, HEAD_PAD), lambda i, j: (j, 0)),
                  pl.BlockSpec((t, HEAD_PAD), lambda i, j: (j, 0)),
                  full(w1), full(p['q_a_norm']), full(p['wqa']), full(p['wqb']),
                  full(p['kv_a_norm']), full(p['wk']), full(p['wv'])],
        out_specs=[tok(na_w), tok(na_w), tok(na_w), tok(hw), tok(hw), tok(hw)],
        out_shape=[jax.ShapeDtypeStruct((b, s, na_w), BF16)] * 3
                  + [jax.ShapeDtypeStruct((b, s, hw), BF16)] * 3,
        compiler_params=_cparams(("arbitrary", "arbitrary")),
        name="inproj",
    )(x, ada3, p['ln_pre_mix'], ck, sk, w1, p['q_a_norm'], p['wqa'], p['wqb'],
      p['kv_a_norm'], p['wk'], p['wv'])
    return outs


def _na_kernel(q_ref, kp_ref, kc_ref, kn_ref, vp_ref, vc_ref, vn_ref, bias_ref, o_ref,
               kcat, vcat, *, rows):
    blk = pl.program_id(1)
    nb = NA_ROWS * GRID_W
    win = NA_KR * GRID_W
    for i, (kr, vr) in enumerate(((kp_ref, vp_ref), (kc_ref, vc_ref), (kn_ref, vn_ref))):
        kcat[i * nb:(i + 1) * nb, :] = kr[0]
        vcat[i * nb:(i + 1) * nb, :] = vr[0]
    lane = lax.broadcasted_iota(jnp.int32, (GRID_W, LANE), 1)
    lo = lane < HEAD_DIM

    def row_body(rl, carry):
        r = blk * NA_ROWS + rl
        rs = jnp.clip(r - NA_KR // 2, 0, rows - NA_KR)
        d = r - rs
        off = pl.multiple_of((rs - (blk - 1) * NA_ROWS) * GRID_W, GRID_W)
        qoff = pl.multiple_of(rl * GRID_W, GRID_W)
        scores = []
        for hp in range(NA_HEADS // 2):
            sl = slice(hp * LANE, (hp + 1) * LANE)
            qp = q_ref[0, pl.ds(qoff, GRID_W), sl]
            kw = kcat[pl.ds(off, win), sl]
            for hh in range(2):
                qm = jnp.where(lo if hh == 0 else jnp.logical_not(lo), qp, jnp.zeros_like(qp))
                s = lax.dot_general(qm, kw, (((1,), (1,)), ((), ())), preferred_element_type=F32)
                scores.append(s + bias_ref[d, hp * 2 + hh])
        probs, inv_l = [], []
        for s in scores:
            m = jnp.max(s, axis=-1, keepdims=True)
            e = jnp.exp(s - m)
            inv_l.append(1.0 / jnp.sum(e, axis=-1, keepdims=True))
            probs.append(e.astype(BF16))
        pair_outs = []
        for hp in range(NA_HEADS // 2):
            vw = vcat[pl.ds(off, win), hp * LANE:(hp + 1) * LANE]
            outs = [jnp.dot(probs[hp * 2 + hh], vw, preferred_element_type=F32) * inv_l[hp * 2 + hh]
                    for hh in range(2)]
            pair_outs.append(jnp.where(lo, outs[0], outs[1]))
        o_ref[0, pl.ds(qoff, GRID_W), :] = jnp.concatenate(pair_outs, axis=1)
        return carry

    lax.fori_loop(0, NA_ROWS, row_body, 0)


def _na_bias_table(rpb):
    nh, nrow, nrel = rpb.shape
    cols = np.arange(GRID_W)
    cstart = np.clip(cols - NA_KC // 2, 0, GRID_W - NA_KC)
    j = np.arange(GRID_W)
    inwin = (j[None, :] >= cstart[:, None]) & (j[None, :] < cstart[:, None] + NA_KC)
    period = 2 * GRID_W - 1
    vpad = jnp.concatenate([rpb[..., NA_KC - 1:], jnp.zeros((nh, nrow, period - nrel), rpb.dtype),
                            rpb[..., :NA_KC - 1]], axis=-1)
    toep = jnp.tile(vpad, (1, 1, GRID_W))[..., :GRID_W * (period - 1)]
    toep = toep.reshape(nh, nrow, GRID_W, period - 1)[..., :GRID_W]
    tabs = []
    for d in range(NA_KR):
        t = toep[:, NA_KR - 1 - d:2 * NA_KR - 1 - d]
        tabs.append(t.transpose(0, 2, 1, 3))
    tab = jnp.stack(tabs, axis=0).astype(F32)
    tab = jnp.where(jnp.asarray(inwin)[None, None, :, None, :], tab, NEG_BIG)
    return tab.reshape(NA_KR, nh, GRID_W, NA_KR * GRID_W)


def _na(q, k, v, bias):
    b, s, w = q.shape
    rows = s // GRID_W
    assert rows % NA_ROWS == 0 and rows >= 2 * NA_ROWS
    nblk = rows // NA_ROWS
    nb = NA_ROWS * GRID_W
    cur = pl.BlockSpec((1, nb, w), lambda i, j: (i, j, 0))
    prv = pl.BlockSpec((1, nb, w), lambda i, j: (i, jnp.maximum(j - 1, 0), 0))
    nxt = pl.BlockSpec((1, nb, w), lambda i, j: (i, jnp.minimum(j + 1, nblk - 1), 0))
    return pl.pallas_call(
        functools.partial(_na_kernel, rows=rows),
        grid=(b, nblk),
        in_specs=[cur, prv, cur, nxt, prv, cur, nxt,
                  pl.BlockSpec(bias.shape, lambda i, j: (0, 0, 0, 0))],
        out_specs=cur,
        out_shape=jax.ShapeDtypeStruct((b, s, w), F32),
        scratch_shapes=[pltpu.VMEM((3 * nb, w), BF16), pltpu.VMEM((3 * nb, w), BF16)],
        compiler_params=_cparams(("arbitrary", "arbitrary")),
        name="na",
    )(q, k, k, k, v, v, v, bias)


def _flash_kernel(q_ref, k_ref, v_ref, o_ref, m_sc, acc_sc, s_sc, *, tk, nk, unroll):
    q = q_ref[0]
    m_sc[...] = jnp.full(m_sc.shape, -jnp.inf, F32)
    acc_sc[...] = jnp.zeros(acc_sc.shape, F32)

    def scores(j):
        off = pl.multiple_of(j * tk, tk)
        ks = k_ref[0, pl.ds(off, tk), :]
        return lax.dot_general(q, ks, (((1,), (1,)), ((), ())), preferred_element_type=F32)

    def accumulate(s, j):
        off = pl.multiple_of(j * tk, tk)
        vs = v_ref[0, pl.ds(off, tk), :]
        cols = [s[:, c * LANE:(c + 1) * LANE] for c in range(tk // LANE)]
        cm = cols[0]
        for sc in cols[1:]:
            cm = jnp.maximum(cm, sc)
        m_prev = m_sc[...]
        m_new = jnp.maximum(m_prev, jnp.max(cm, axis=-1, keepdims=True))
        alpha = jnp.exp2(m_prev - m_new)
        p = jnp.concatenate([jnp.exp2((sc - m_new).astype(BF16)) for sc in cols], axis=1)
        acc_sc[...] = alpha * acc_sc[...] + jnp.dot(p, vs, preferred_element_type=F32)
        m_sc[...] = m_new

    s_sc[...] = scores(0)

    def trip(i, carry):
        s_cur = s_sc[...]
        for u in range(unroll):
            j = i * unroll + u
            s_next = scores(jnp.minimum(j + 1, nk - 1))
            accumulate(s_cur, j)
            s_cur = s_next
        s_sc[...] = s_cur
        return carry

    lax.fori_loop(0, nk // unroll, trip, 0)
    acc = acc_sc[...]
    o_ref[0] = acc / acc[:, MLA_V:MLA_V + 1]


def _flash(q, k, v):
    b, s, hw = q.shape
    nh = hw // HEAD_PAD
    tq = min(T_Q, s)
    tk = min(T_K, s)
    nk = s // tk
    unroll = 4 if nk % 4 == 0 else (2 if nk % 2 == 0 else 1)
    return pl.pallas_call(
        functools.partial(_flash_kernel, tk=tk, nk=nk, unroll=unroll),
        grid=(b, nh, s // tq),
        in_specs=[pl.BlockSpec((1, tq, HEAD_PAD), lambda i, h, j: (i, j, h)),
                  pl.BlockSpec((1, s, HEAD_PAD), lambda i, h, j: (i, 0, h)),
                  pl.BlockSpec((1, s, HEAD_PAD), lambda i, h, j: (i, 0, h))],
        out_specs=pl.BlockSpec((1, tq, HEAD_PAD), lambda i, h, j: (i, j, h)),
        out_shape=jax.ShapeDtypeStruct((b, s, hw), F32),
        scratch_shapes=[pltpu.VMEM((tq, HEAD_PAD), F32), pltpu.VMEM((tq, HEAD_PAD), F32),
                        pltpu.VMEM((tq, tk), F32)],
        compiler_params=_cparams(("arbitrary", "arbitrary", "arbitrary")),
        name="flash",
    )(q, k, v)


def _post_kernel(na_ref, mla_ref, x_ref, ada_ref, gna_ref, gmla_ref, wona_ref, womla_ref,
                 lnpost_ref, lnpre_ref, wr_ref, br_ref,
                 x1_ref, h2_ref, e_ref, g_ref, rank_ref, cnt_ref, base_sc):
    first = jnp.logical_and(pl.program_id(0) == 0, pl.program_id(1) == 0)

    @pl.when(first)
    def _():
        base_sc[...] = jnp.zeros(base_sc.shape, F32)

    t = x_ref.shape[1]
    na = na_ref[0]
    nan_ = _rms(na) * gna_ref[...]
    ml = mla_ref[0]
    lane_w = lax.broadcasted_iota(jnp.int32, ml.shape, 1)
    ml = jnp.where((lane_w % HEAD_PAD) < MLA_V, ml, 0.0)
    mln = _rms(ml, n=MLA_HEADS * MLA_V) * gmla_ref[...]
    mix = (jnp.dot(nan_.astype(BF16), wona_ref[...], preferred_element_type=F32)
           + jnp.dot(mln.astype(BF16), womla_ref[...], preferred_element_type=F32))
    g_a = ada_ref[0, 2:3, :]
    sh_m = ada_ref[0, 3:4, :]
    sc_m = ada_ref[0, 4:5, :]
    x1 = x_ref[0] + g_a * (_rms(mix) * lnpost_ref[...])
    x1_ref[0] = x1
    h2 = (_rms(x1) * lnpre_ref[...]) * (1.0 + sc_m) + sh_m
    h2_ref[0] = h2

    logits = jnp.dot(h2, wr_ref[...], preferred_element_type=F32,
                     precision=lax.Precision.HIGHEST) + br_ref[...]
    lane = lax.broadcasted_iota(jnp.int32, (t, LANE), 1)
    lane_f = lane.astype(F32)
    work = jnp.where(lane < N_EXPERTS, logits, -jnp.inf)
    vals, idxs = [], []
    onehot = jnp.zeros((t, LANE), F32)
    for _ in range(TOP_K):
        mk = jnp.max(work, axis=-1, keepdims=True)
        ik = jnp.min(jnp.where(work == mk, lane_f, float(LANE)), axis=-1, keepdims=True)
        sel = lane_f == ik
        work = jnp.where(sel, -jnp.inf, work)
        onehot = jnp.where(sel, 1.0, onehot)
        vals.append(mk)
        idxs.append(ik)
    exps = [jnp.exp(vk - vals[0]) for vk in vals]
    denom = exps[0]
    for ek in exps[1:]:
        denom = denom + ek

    r_i = lax.broadcasted_iota(jnp.int32, (t, t), 0)
    c_i = lax.broadcasted_iota(jnp.int32, (t, t), 1)
    tril = jnp.where(c_i < r_i, 1.0, 0.0).astype(BF16)
    pref = jnp.dot(tril, onehot.astype(BF16), preferred_element_type=F32) + base_sc[...]

    e_out = jnp.zeros((t, LANE), F32)
    g_out = jnp.zeros((t, LANE), F32)
    r_out = jnp.zeros((t, LANE), F32)
    for kk in range(TOP_K):
        rk = jnp.sum(jnp.where(lane_f == idxs[kk], pref, 0.0), axis=-1, keepdims=True)
        here = lane == kk
        e_out = jnp.where(here, idxs[kk], e_out)
        g_out = jnp.where(here, exps[kk] / denom, g_out)
        r_out = jnp.where(here, rk, r_out)
    e_ref[0] = e_out.astype(jnp.int32)
    g_ref[0] = g_out
    rank_ref[0] = r_out.astype(jnp.int32)
    base_sc[...] = base_sc[...] + jnp.sum(onehot, axis=0, keepdims=True)
    cnt_ref[...] = jnp.broadcast_to(base_sc[...], cnt_ref.shape)


def _post(na_out, mla_out, x, ada3, p):
    b, s, d = x.shape
    t = min(T_POST, s)
    full = lambda a: pl.BlockSpec(a.shape, lambda i, j: (0,) * a.ndim)
    tok = lambda w: pl.BlockSpec((1, t, w), lambda i, j: (i, j, 0))
    return pl.pallas_call(
        _post_kernel,
        grid=(b, s // t),
        in_specs=[tok(na_out.shape[-1]), tok(mla_out.shape[-1]), tok(d),
                  pl.BlockSpec((1, 6, d), lambda i, j: (i, 0, 0)),
                  full(p['g_na_out']), full(p['g_mla_pad']), full(p['wo_na']), full(p['wo_mla']),
                  full(p['ln_post_mix']), full(p['ln_pre_moe']), full(p['w_router']), full(p['b_router'])],
        out_specs=[tok(d), tok(d), tok(LANE), tok(LANE), tok(LANE),
                   pl.BlockSpec((8, LANE), lambda i, j: (0, 0))],
        out_shape=[jax.ShapeDtypeStruct((b, s, d), F32), jax.ShapeDtypeStruct((b, s, d), F32),
                   jax.ShapeDtypeStruct((b, s, LANE), jnp.int32), jax.ShapeDtypeStruct((b, s, LANE), F32),
                   jax.ShapeDtypeStruct((b, s, LANE), jnp.int32), jax.ShapeDtypeStruct((8, LANE), F32)],
        scratch_shapes=[pltpu.VMEM((1, LANE), F32)],
        compiler_params=_cparams(("arbitrary", "arbitrary")),
        name="post",
    )(na_out, mla_out, x, ada3, p['g_na_out'], p['g_mla_pad'], p['wo_na'], p['wo_mla'],
      p['ln_post_mix'], p['ln_pre_moe'], p['w_router'], p['b_router'])


def _experts_kernel(be_ref, nact_ref, tok_hbm, h_hbm, w1g_ref, w1l_ref, w2_ref, b1g_ref, b1l_ref, b2_ref,
                    y_ref, idx_sm, xbuf, isem, rsem, *, blk):
    i = pl.program_id(0)
    nact = nact_ref[0]

    def idx_copy(step, slot):
        return pltpu.make_async_copy(tok_hbm.at[step], idx_sm.at[slot], isem.at[slot])

    def issue_rows(slot):
        def one(j, carry):
            tk = idx_sm[slot, j]
            pltpu.make_async_copy(h_hbm.at[pl.ds(tk, 1), :], xbuf.at[slot, pl.ds(j, 1), :],
                                  rsem.at[slot]).start()
            return carry
        lax.fori_loop(0, blk, one, 0)

    def wait_rows(slot):
        pltpu.make_async_copy(h_hbm.at[pl.ds(0, blk), :], xbuf.at[slot], rsem.at[slot]).wait()

    @pl.when(jnp.logical_and(i == 0, nact > 0))
    def _():
        idx_copy(0, 0).start()
        idx_copy(0, 0).wait()
        issue_rows(0)

        @pl.when(nact > 1)
        def _():
            idx_copy(1, 1).start()

    cur = lax.rem(i, 2)
    nxt = 1 - cur

    @pl.when(i + 1 < nact)
    def _():
        idx_copy(i + 1, nxt).wait()
        issue_rows(nxt)

        @pl.when(i + 2 < nact)
        def _():
            idx_copy(i + 2, cur).start()

    @pl.when(i < nact)
    def _():
        wait_rows(cur)
        x = xbuf[cur].astype(BF16)
        ug = jnp.dot(x, w1g_ref[0], preferred_element_type=F32) + b1g_ref[0]
        ul = jnp.dot(x, w1l_ref[0], preferred_element_type=F32) + b1l_ref[0]
        x_glu = jnp.minimum(ug, SWIGLU_LIMIT)
        x_lin = jnp.clip(ul, -SWIGLU_LIMIT, SWIGLU_LIMIT)
        act = (x_lin + 1.0) * (x_glu * (1.0 / (1.0 + jnp.exp(-SWIGLU_ALPHA * x_glu))))
        y_ref[...] = jnp.dot(act.astype(BF16), w2_ref[0], preferred_element_type=F32) + b2_ref[0]

    @pl.when(i >= nact)
    def _():
        y_ref[...] = jnp.zeros(y_ref.shape, F32)


def _experts(h2, slot_tok, block_e, nact, p):
    n, d = h2.shape
    nb, blk = slot_tok.shape
    f = p['w1g'].shape[2]
    wspec = lambda a: pl.BlockSpec((1,) + a.shape[1:], lambda i, be, na: (be[i], 0, 0))
    grid_spec = pltpu.PrefetchScalarGridSpec(
        num_scalar_prefetch=2,
        grid=(nb,),
        in_specs=[pl.BlockSpec(memory_space=pl.ANY), pl.BlockSpec(memory_space=pl.ANY),
                  wspec(p['w1g']), wspec(p['w1l']), wspec(p['w2']),
                  wspec(p['b1g']), wspec(p['b1l']), wspec(p['b2'])],
        out_specs=pl.BlockSpec((blk, d), lambda i, be, na: (i, 0)),
        scratch_shapes=[pltpu.SMEM((2, blk), jnp.int32), pltpu.VMEM((2, blk, d), F32),
                        pltpu.SemaphoreType.DMA((2,)), pltpu.SemaphoreType.DMA((2,))],
    )
    return pl.pallas_call(
        functools.partial(_experts_kernel, blk=blk),
        grid_spec=grid_spec,
        out_shape=jax.ShapeDtypeStruct((nb * blk, d), F32),
        compiler_params=_cparams(("arbitrary",)),
        name="experts",
    )(block_e, nact, slot_tok, h2, p['w1g'], p['w1l'], p['w2'], p['b1g'], p['b1l'], p['b2'])


def _deint_kernel(w_ref, pg_ref, pl_ref, g_ref, l_ref):
    w = w_ref[0].astype(BF16)
    g_ref[0] = jnp.dot(w, pg_ref[...], preferred_element_type=F32).astype(BF16)
    l_ref[0] = jnp.dot(w, pl_ref[...], preferred_element_type=F32).astype(BF16)


def _deinterleave_w1(w_mlp1):
    ne, d, f2 = w_mlp1.shape
    f = f2 // 2
    rows = min(512, d)
    src = lax.broadcasted_iota(jnp.int32, (f2, f), 0)
    dst = lax.broadcasted_iota(jnp.int32, (f2, f), 1)
    pg = (src == 2 * dst).astype(BF16)
    pl_ = (src == 2 * dst + 1).astype(BF16)
    return pl.pallas_call(
        _deint_kernel,
        grid=(ne, d // rows),
        in_specs=[pl.BlockSpec((1, rows, f2), lambda e, r: (e, r, 0)),
                  pl.BlockSpec((f2, f), lambda e, r: (0, 0)),
                  pl.BlockSpec((f2, f), lambda e, r: (0, 0))],
        out_specs=[pl.BlockSpec((1, rows, f), lambda e, r: (e, r, 0))] * 2,
        out_shape=[jax.ShapeDtypeStruct((ne, d, f), BF16)] * 2,
        compiler_params=_cparams(("arbitrary", "arbitrary")),
        name="deint",
    )(w_mlp1, pg, pl_)


def _combine_kernel(dest_hbm, y_hbm, x1_ref, g_ref, ada_ref, ln_ref, o_ref,
                    idx_sm, ybuf, isem, rsem, *, t, nsteps):
    i = pl.program_id(0) * pl.num_programs(1) + pl.program_id(1)

    def idx_copy(step, slot):
        return pltpu.make_async_copy(dest_hbm.at[step], idx_sm.at[slot], isem.at[slot])

    def issue_rows(slot):
        def one(j, carry):
            for kk in range(TOP_K):
                ds_ = idx_sm[slot, j * TOP_K + kk]
                pltpu.make_async_copy(y_hbm.at[pl.ds(ds_, 1), :], ybuf.at[slot, kk, pl.ds(j, 1), :],
                                      rsem.at[slot]).start()
            return carry
        lax.fori_loop(0, t, one, 0)

    def wait_rows(slot):
        for kk in range(TOP_K):
            pltpu.make_async_copy(y_hbm.at[pl.ds(0, t), :], ybuf.at[slot, kk], rsem.at[slot]).wait()

    @pl.when(i == 0)
    def _():
        idx_copy(0, 0).start()
        idx_copy(0, 0).wait()
        issue_rows(0)
        if nsteps > 1:
            idx_copy(1, 1).start()

    cur = lax.rem(i, 2)
    nxt = 1 - cur

    @pl.when(i + 1 < nsteps)
    def _():
        idx_copy(i + 1, nxt).wait()
        issue_rows(nxt)

        @pl.when(i + 2 < nsteps)
        def _():
            idx_copy(i + 2, cur).start()

    wait_rows(cur)
    g = g_ref[0]
    y = jnp.zeros((t, x1_ref.shape[2]), F32)
    for kk in range(TOP_K):
        y = y + ybuf[cur, kk] * g[:, kk:kk + 1]
    g_m = ada_ref[0, 5:6, :]
    o_ref[0] = x1_ref[0] + g_m * (_rms(y) * ln_ref[...])


def _combine(dest, y_sorted, x1, gates, ada3, ln_post_moe):
    b, s, d = x1.shape
    t = min(T_COMB, s)
    ns = s // t
    nsteps = b * ns
    dest2 = dest.reshape(nsteps, t * TOP_K)
    return pl.pallas_call(
        functools.partial(_combine_kernel, t=t, nsteps=nsteps),
        grid=(b, ns),
        in_specs=[pl.BlockSpec(memory_space=pl.ANY), pl.BlockSpec(memory_space=pl.ANY),
                  pl.BlockSpec((1, t, d), lambda i, j: (i, j, 0)),
                  pl.BlockSpec((1, t, LANE), lambda i, j: (i, j, 0)),
                  pl.BlockSpec((1, 6, d), lambda i, j: (i, 0, 0)),
                  pl.BlockSpec(ln_post_moe.shape, lambda i, j: (0, 0))],
        out_specs=pl.BlockSpec((1, t, d), lambda i, j: (i, j, 0)),
        out_shape=jax.ShapeDtypeStruct((b, s, d), F32),
        scratch_shapes=[pltpu.SMEM((2, t * TOP_K), jnp.int32), pltpu.VMEM((2, TOP_K, t, d), F32),
                        pltpu.SemaphoreType.DMA((2,)), pltpu.SemaphoreType.DMA((2,))],
        compiler_params=_cparams(("arbitrary", "arbitrary")),
        name="combine",
    )(dest2, y_sorted, x1, gates, ada3, ln_post_moe)


def _rope_partner(w):
    half = MLA_ROPE // 2
    return jnp.concatenate([-w[..., half:], w[..., :half]], axis=-1)


def _prep_params(ln_pre_mix, ln_post_mix, ln_pre_moe, ln_post_moe, w_in, na_rpb, q_a_norm, w_q_b,
                 kv_a_norm, w_kv_b, g_na_out, g_mla_out, w_o, w_router, b_router,
                 w_mlp1, b_mlp1, w_mlp2, b_mlp2):
    d = w_in.shape[0]
    na_w = NA_HEADS * HEAD_DIM
    q_lora = w_q_b.shape[0]
    kv_lora = w_kv_b.shape[0]
    qk = MLA_NOPE + MLA_ROPE
    main_w = 3 * na_w + q_lora + kv_lora
    w_kr = w_in[:, main_w:main_w + MLA_ROPE]
    zpad = lambda n: jnp.zeros((d, n), F32)
    kra = jnp.concatenate([zpad(MLA_NOPE), w_kr, zpad(HEAD_PAD - qk)], axis=1)
    krb = jnp.concatenate([zpad(MLA_NOPE), _rope_partner(w_kr), zpad(HEAD_PAD - qk)], axis=1)
    w1 = jnp.concatenate([w_in[:, :main_w], kra, krb], axis=1).astype(BF16)

    wq = w_q_b.reshape(q_lora, MLA_HEADS, qk)
    zq = lambda n: jnp.zeros((q_lora, MLA_HEADS, n), F32)
    wqa = jnp.concatenate([wq, zq(HEAD_PAD - qk)], axis=2).reshape(q_lora, -1).astype(BF16)
    wqb = jnp.concatenate([zq(MLA_NOPE), _rope_partner(wq[:, :, MLA_NOPE:]), zq(HEAD_PAD - qk)],
                          axis=2).reshape(q_lora, -1).astype(BF16)
    wkv = w_kv_b.reshape(kv_lora, MLA_HEADS, MLA_NOPE + MLA_V)
    zk = lambda n: jnp.zeros((kv_lora, MLA_HEADS, n), F32)
    wk = jnp.concatenate([wkv[:, :, :MLA_NOPE], zk(HEAD_PAD - MLA_NOPE)], axis=2).reshape(kv_lora, -1).astype(BF16)
    wv = jnp.concatenate([wkv[:, :, MLA_NOPE:], zk(HEAD_PAD - MLA_V)], axis=2).reshape(kv_lora, -1).astype(BF16)

    mla_w = MLA_HEADS * MLA_V
    g_mla_pad = jnp.concatenate([g_mla_out.reshape(MLA_HEADS, MLA_V),
                                 jnp.zeros((MLA_HEADS, HEAD_PAD - MLA_V), F32)], axis=1).reshape(1, -1)
    wo_mla = w_o[na_w:na_w + mla_w].reshape(MLA_HEADS, MLA_V, d)
    wo_mla = jnp.concatenate([wo_mla, jnp.zeros((MLA_HEADS, HEAD_PAD - MLA_V, d), F32)],
                             axis=1).reshape(MLA_HEADS * HEAD_PAD, d).astype(BF16)
    w_r = jnp.concatenate([w_router, jnp.zeros((d, LANE - N_EXPERTS), F32)], axis=1)
    b_r = jnp.concatenate([b_router, jnp.zeros((LANE - N_EXPERTS,), F32)]).reshape(1, LANE)
    ne = w_mlp1.shape[0]
    w1g, w1l = _deinterleave_w1(w_mlp1)
    return {
        'ln_pre_mix': ln_pre_mix.reshape(1, d), 'ln_post_mix': ln_post_mix.reshape(1, d),
        'ln_pre_moe': ln_pre_moe.reshape(1, d), 'ln_post_moe': ln_post_moe.reshape(1, d),
        'w1': w1, 'q_a_norm': q_a_norm.reshape(1, q_lora), 'wqa': wqa, 'wqb': wqb,
        'kv_a_norm': kv_a_norm.reshape(1, kv_lora), 'wk': wk, 'wv': wv,
        'na_bias': _na_bias_table(na_rpb),
        'g_na_out': g_na_out.reshape(1, na_w), 'g_mla_pad': g_mla_pad,
        'wo_na': w_o[:na_w].astype(BF16), 'wo_mla': wo_mla,
        'w_router': w_r, 'b_router': b_r,
        'w1g': w1g, 'w1l': w1l,
        'w2': w_mlp2.astype(BF16),
        'b1g': b_mlp1[:, 0::2].reshape(ne, 1, -1), 'b1l': b_mlp1[:, 1::2].reshape(ne, 1, -1),
        'b2': b_mlp2.reshape(ne, 1, -1),
    }


def _rope_lane_tables(s):
    half = MLA_ROPE // 2
    inv = ROPE_THETA ** (-jnp.arange(half, dtype=F32) / half)
    ang = jnp.arange(s, dtype=F32)[:, None] * inv[None, :]
    cos, sin = jnp.cos(ang), jnp.sin(ang)
    z = lambda n: jnp.zeros((s, n), F32)
    tail = HEAD_PAD - MLA_NOPE - MLA_ROPE
    ck = jnp.concatenate([z(MLA_NOPE), cos, cos, z(tail)], axis=1)
    sk = jnp.concatenate([z(MLA_NOPE), sin, sin, z(tail)], axis=1)
    return ck, sk


def _moe_plan(e4, rank4, counts, n):
    nk = n * TOP_K
    padded = (counts + MOE_BLK - 1) // MOE_BLK * MOE_BLK
    pad_end = jnp.cumsum(padded)
    pad_start = pad_end - padded
    dest = pad_start[e4] + rank4
    nb = (nk + N_EXPERTS * (MOE_BLK - 1) + MOE_BLK - 1) // MOE_BLK
    tok = jnp.broadcast_to(jnp.arange(n, dtype=jnp.int32)[:, None], (n, TOP_K))
    slot_tok = jnp.zeros((nb * MOE_BLK,), jnp.int32).at[dest.reshape(-1)].set(tok.reshape(-1))
    block_start = jnp.arange(nb, dtype=jnp.int32) * MOE_BLK
    block_e = jnp.minimum(jnp.searchsorted(pad_end, block_start, side='right'),
                          N_EXPERTS - 1).astype(jnp.int32)
    nact = (pad_end[-1] // MOE_BLK).astype(jnp.int32).reshape(1)
    return dest.astype(jnp.int32), slot_tok.reshape(nb, MOE_BLK), block_e, nact


def _encoder_layer(x, ada3, p):
    b, s, d = x.shape
    ck, sk = _rope_lane_tables(s)
    qna, kna, vna, q, k, v = _inproj(x, ada3, p, ck, sk)
    na_out = _na(qna, kna, vna, p['na_bias'])
    mla_out = _flash(q, k, v)
    x1, h2, e, g, rank, cnt = _post(na_out, mla_out, x, ada3, p)
    n = b * s
    e4 = e.reshape(n, LANE)[:, :TOP_K]
    rank4 = rank.reshape(n, LANE)[:, :TOP_K]
    counts = cnt[0, :N_EXPERTS].astype(jnp.int32)
    dest, slot_tok, block_e, nact = _moe_plan(e4, rank4, counts, n)
    y_sorted = _experts(h2.reshape(n, d), slot_tok, block_e, nact, p)
    return _combine(dest, y_sorted, x1, g, ada3, p['ln_post_moe'])


def kernel(x_prompt, x_sample, c_prompt, c_sample, ln_pre_mix, ln_post_mix, ln_pre_moe, ln_post_moe, w_ada, b_ada, w_in, na_rpb, q_a_norm, w_q_b, kv_a_norm, w_kv_b, g_na_out, g_mla_out, w_o, w_router, b_router, w_mlp1, b_mlp1, w_mlp2, b_mlp2):
    depth = w_ada.shape[0]
    d = x_prompt.shape[-1]
    bp, bs = x_prompt.shape[0], x_sample.shape[0]
    rows = -(-(bp + bs) // 8) * 8
    y_prompt, y_sample = x_prompt, x_sample
    for l in range(depth):
        p = _prep_params(ln_pre_mix[l], ln_post_mix[l], ln_pre_moe[l], ln_post_moe[l], w_in[l], na_rpb[l],
                         q_a_norm[l], w_q_b[l], kv_a_norm[l], w_kv_b[l], g_na_out[l], g_mla_out[l], w_o[l],
                         w_router[l], b_router[l], w_mlp1[l], b_mlp1[l], w_mlp2[l], b_mlp2[l])
        c_all = jnp.concatenate([c_prompt, c_sample, jnp.zeros((rows - bp - bs, d), F32)], axis=0)
        ada3 = _ada(c_all, w_ada[l], b_ada[l]).reshape(rows, 6, d)
        y_prompt = _encoder_layer(y_prompt, ada3[:bp], p)
        y_sample = _encoder_layer(y_sample, ada3[bp:bp + bs], p)
    return (y_prompt, y_sample)
```

```python
import functools

import numpy as np
import jax
import jax.numpy as jnp
from jax import lax
from jax.experimental import pallas as pl
from jax.experimental.pallas import tpu as pltpu

F32 = jnp.float32
BF16 = jnp.bfloat16

GRID_W = 64
HEAD_DIM = 64
NA_HEADS = 8
NA_KR = 8
NA_KC = 16
MLA_HEADS = 8
MLA_NOPE = 64
MLA_ROPE = 32
MLA_V = 64
ROPE_THETA = 10000.0
N_EXPERTS = 32
TOP_K = 4
SWIGLU_LIMIT = 7.0
SWIGLU_ALPHA = 1.702
EPS = 1e-6

LANE = 128
HEAD_PAD = 128
NEG_BIG = -1e30
LOG2E = 1.4426950408889634

T_PROJ = 256
T_Q = 512
T_K = 512
NA_ROWS = 8
T_POST = 256
MOE_BLK = 512
T_COMB = 128
VMEM_LIMIT = 56 * 1024 * 1024


def _rms(x, n=None):
    n = x.shape[-1] if n is None else n
    return x * lax.rsqrt(jnp.sum(x * x, axis=-1, keepdims=True) * (1.0 / n) + EPS)


def _cparams(sem):
    return pltpu.CompilerParams(dimension_semantics=sem, vmem_limit_bytes=VMEM_LIMIT)


def _store_token_tiles(ref, x):
    rows, d = x.shape
    nc = d // LANE
    for c in range(nc):
        ref[pl.ds(c, rows, stride=nc), :] = x[:, c * LANE:(c + 1) * LANE]


def _load_token_tiles(ref, nc):
    rows = ref.shape[0] // nc
    return jnp.concatenate([ref[pl.ds(c, rows, stride=nc), :] for c in range(nc)], axis=1)


def _ada_kernel(c_ref, w_ref, b_ref, o_ref):
    c = c_ref[...]
    s = c / (1.0 + jnp.exp(-c))
    o_ref[...] = jnp.dot(s, w_ref[...], preferred_element_type=F32,
                         precision=lax.Precision.HIGHEST) + b_ref[...]


def _ada(c_all, w_ada, b_ada):
    rows, d = c_all.shape
    n_out = w_ada.shape[1]
    return pl.pallas_call(
        _ada_kernel,
        grid=(n_out // d,),
        in_specs=[pl.BlockSpec((rows, d), lambda j: (0, 0)),
                  pl.BlockSpec((d, d), lambda j: (0, j)),
                  pl.BlockSpec((1, d), lambda j: (0, j))],
        out_specs=pl.BlockSpec((rows, d), lambda j: (0, j)),
        out_shape=jax.ShapeDtypeStruct((rows, n_out), F32),
        compiler_params=_cparams(("arbitrary",)),
        name="ada",
    )(c_all, w_ada, b_ada.reshape(1, n_out))


def _inproj_kernel(x_ref, ada_ref, g_ref, ck_ref, sk_ref, w1_ref, qan_ref, wqa_ref, wqb_ref,
                   kvan_ref, wk_ref, wv_ref,
                   qna_ref, kna_ref, vna_ref, q_ref, k_ref, v_ref, *, na_w, q_lora, kv_lora, qscale):
    x = x_ref[0]
    sh = ada_ref[0, 0:1, :]
    sc = ada_ref[0, 1:2, :]
    h = (_rms(x) * g_ref[...]) * (1.0 + sc) + sh
    proj = jnp.dot(h.astype(BF16), w1_ref[...], preferred_element_type=F32)
    qna_ref[0] = (proj[:, 0:na_w] * (1.0 / float(np.sqrt(HEAD_DIM)))).astype(BF16)
    kna_ref[0] = proj[:, na_w:2 * na_w].astype(BF16)
    vna_ref[0] = proj[:, 2 * na_w:3 * na_w].astype(BF16)
    o = 3 * na_w
    cqn = (_rms(proj[:, o:o + q_lora]) * qan_ref[...]).astype(BF16)
    o += q_lora
    ckvn = (_rms(proj[:, o:o + kv_lora]) * kvan_ref[...]).astype(BF16)
    o += kv_lora
    kra = proj[:, o:o + HEAD_PAD]
    krb = proj[:, o + HEAD_PAD:o + 2 * HEAD_PAD]
    qa = jnp.dot(cqn, wqa_ref[...], preferred_element_type=F32)
    qb = jnp.dot(cqn, wqb_ref[...], preferred_element_type=F32)
    kn = jnp.dot(ckvn, wk_ref[...], preferred_element_type=F32)
    vv = jnp.dot(ckvn, wv_ref[...], preferred_element_type=F32)
    ck = ck_ref[...]
    sk = sk_ref[...]
    lane = lax.broadcasted_iota(jnp.int32, (1, HEAD_PAD), 1)
    nope = (lane < MLA_NOPE).astype(F32)
    ones_col = (lane == MLA_V).astype(F32)
    cq = (nope + ck) * qscale
    sq = sk * qscale
    kpe = kra * ck + krb * sk
    for hd in range(MLA_HEADS):
        sl = slice(hd * HEAD_PAD, (hd + 1) * HEAD_PAD)
        q_ref[0, :, sl] = (qa[:, sl] * cq + qb[:, sl] * sq).astype(BF16)
        k_ref[0, :, sl] = (kn[:, sl] + kpe).astype(BF16)
        v_ref[0, :, sl] = (vv[:, sl] + ones_col).astype(BF16)


def _inproj(x, ada3, p, ck, sk):
    b, s, d = x.shape
    t = min(T_PROJ, s)
    na_w = NA_HEADS * HEAD_DIM
    hw = MLA_HEADS * HEAD_PAD
    q_lora = p['wqa'].shape[0]
    kv_lora = p['wk'].shape[0]
    w1 = p['w1']
    kern = functools.partial(_inproj_kernel, na_w=na_w, q_lora=q_lora, kv_lora=kv_lora,
                             qscale=LOG2E / float(np.sqrt(MLA_NOPE + MLA_ROPE)))
    full = lambda a: pl.BlockSpec(a.shape, lambda i, j: (0,) * a.ndim)
    tok = lambda w: pl.BlockSpec((1, t, w), lambda i, j: (i, j, 0))
    outs = pl.pallas_call(
        kern,
        grid=(b, s // t),
        in_specs=[tok(d),
                  pl.BlockSpec((1, 6, d), lambda i, j: (i, 0, 0)),
                  full(p['ln_pre_mix']),
                  pl.BlockSpec((t, HEAD_PAD), lambda i, j: (j, 0)),
                  pl.BlockSpec((t, HEAD_PAD), lambda i, j: (j, 0)),
                  full(w1), full(p['q_a_norm']), full(p['wqa']), full(p['wqb']),
                  full(p['kv_a_norm']), full(p['wk']), full(p['wv'])],
        out_specs=[tok(na_w), tok(na_w), tok(na_w), tok(hw), tok(hw), tok(hw)],
        out_shape=[jax.ShapeDtypeStruct((b, s, na_w), BF16)] * 3
                  + [jax.ShapeDtypeStruct((b, s, hw), BF16)] * 3,
        compiler_params=_cparams(("arbitrary", "arbitrary")),
        name="inproj",
    )(x, ada3, p['ln_pre_mix'], ck, sk, w1, p['q_a_norm'], p['wqa'], p['wqb'],
      p['kv_a_norm'], p['wk'], p['wv'])
    return outs


def _na_kernel(q_ref, kp_ref, kc_ref, kn_ref, vp_ref, vc_ref, vn_ref, bias_ref, o_ref,
               kcat, vcat, *, rows):
    blk = pl.program_id(1)
    nb = NA_ROWS * GRID_W
    win = NA_KR * GRID_W
    for i, (kr, vr) in enumerate(((kp_ref, vp_ref), (kc_ref, vc_ref), (kn_ref, vn_ref))):
        kcat[i * nb:(i + 1) * nb, :] = kr[0]
        vcat[i * nb:(i + 1) * nb, :] = vr[0]
    lane = lax.broadcasted_iota(jnp.int32, (GRID_W, LANE), 1)
    lo = lane < HEAD_DIM

    def row_body(rl, carry):
        r = blk * NA_ROWS + rl
        rs = jnp.clip(r - NA_KR // 2, 0, rows - NA_KR)
        d = r - rs
        off = pl.multiple_of((rs - (blk - 1) * NA_ROWS) * GRID_W, GRID_W)
        qoff = pl.multiple_of(rl * GRID_W, GRID_W)
        scores = []
        for hp in range(NA_HEADS // 2):
            sl = slice(hp * LANE, (hp + 1) * LANE)
            qp = q_ref[0, pl.ds(qoff, GRID_W), sl]
            kw = kcat[pl.ds(off, win), sl]
            for hh in range(2):
                qm = jnp.where(lo if hh == 0 else jnp.logical_not(lo), qp, jnp.zeros_like(qp))
                s = lax.dot_general(qm, kw, (((1,), (1,)), ((), ())), preferred_element_type=F32)
                scores.append(s + bias_ref[d, hp * 2 + hh])
        probs, inv_l = [], []
        for s in scores:
            m = jnp.max(s, axis=-1, keepdims=True)
            e = jnp.exp(s - m)
            inv_l.append(1.0 / jnp.sum(e, axis=-1, keepdims=True))
            probs.append(e.astype(BF16))
        pair_outs = []
        for hp in range(NA_HEADS // 2):
            vw = vcat[pl.ds(off, win), hp * LANE:(hp + 1) * LANE]
            outs = [jnp.dot(probs[hp * 2 + hh], vw, preferred_element_type=F32) * inv_l[hp * 2 + hh]
                    for hh in range(2)]
            pair_outs.append(jnp.where(lo, outs[0], outs[1]))
        o_ref[0, pl.ds(qoff, GRID_W), :] = jnp.concatenate(pair_outs, axis=1)
        return carry

    lax.fori_loop(0, NA_ROWS, row_body, 0)


def _na_bias_table(rpb):
    nh, nrow, nrel = rpb.shape
    cols = np.arange(GRID_W)
    cstart = np.clip(cols - NA_KC // 2, 0, GRID_W - NA_KC)
    j = np.arange(GRID_W)
    inwin = (j[None, :] >= cstart[:, None]) & (j[None, :] < cstart[:, None] + NA_KC)
    period = 2 * GRID_W - 1
    vpad = jnp.concatenate([rpb[..., NA_KC - 1:], jnp.zeros((nh, nrow, period - nrel), rpb.dtype),
                            rpb[..., :NA_KC - 1]], axis=-1)
    toep = jnp.tile(vpad, (1, 1, GRID_W))[..., :GRID_W * (period - 1)]
    toep = toep.reshape(nh, nrow, GRID_W, period - 1)[..., :GRID_W]
    tabs = []
    for d in range(NA_KR):
        t = toep[:, NA_KR - 1 - d:2 * NA_KR - 1 - d]
        tabs.append(t.transpose(0, 2, 1, 3))
    tab = jnp.stack(tabs, axis=0).astype(F32)
    tab = jnp.where(jnp.asarray(inwin)[None, None, :, None, :], tab, NEG_BIG)
    return tab.reshape(NA_KR, nh, GRID_W, NA_KR * GRID_W)


def _na(q, k, v, bias):
    b, s, w = q.shape
    rows = s // GRID_W
    assert rows % NA_ROWS == 0 and rows >= 2 * NA_ROWS
    nblk = rows // NA_ROWS
    nb = NA_ROWS * GRID_W
    cur = pl.BlockSpec((1, nb, w), lambda i, j: (i, j, 0))
    prv = pl.BlockSpec((1, nb, w), lambda i, j: (i, jnp.maximum(j - 1, 0), 0))
    nxt = pl.BlockSpec((1, nb, w), lambda i, j: (i, jnp.minimum(j + 1, nblk - 1), 0))
    return pl.pallas_call(
        functools.partial(_na_kernel, rows=rows),
        grid=(b, nblk),
        in_specs=[cur, prv, cur, nxt, prv, cur, nxt,
                  pl.BlockSpec(bias.shape, lambda i, j: (0, 0, 0, 0))],
        out_specs=cur,
        out_shape=jax.ShapeDtypeStruct((b, s, w), F32),
        scratch_shapes=[pltpu.VMEM((3 * nb, w), BF16), pltpu.VMEM((3 * nb, w), BF16)],
        compiler_params=_cparams(("arbitrary", "arbitrary")),
        name="na",
    )(q, k, k, k, v, v, v, bias)


def _flash_kernel(q_ref, k_ref, v_ref, o_ref, m_sc, acc_sc, s_sc, *, tk, nk, unroll):
    q = q_ref[0]
    m_sc[...] = jnp.full(m_sc.shape, -jnp.inf, F32)
    acc_sc[...] = jnp.zeros(acc_sc.shape, F32)

    def scores(j):
        off = pl.multiple_of(j * tk, tk)
        ks = k_ref[0, pl.ds(off, tk), :]
        return lax.dot_general(q, ks, (((1,), (1,)), ((), ())), preferred_element_type=F32)

    def accumulate(s, j):
        off = pl.multiple_of(j * tk, tk)
        vs = v_ref[0, pl.ds(off, tk), :]
        cols = [s[:, c * LANE:(c + 1) * LANE] for c in range(tk // LANE)]
        cm = cols[0]
        for sc in cols[1:]:
            cm = jnp.maximum(cm, sc)
        m_prev = m_sc[...]
        m_new = jnp.maximum(m_prev, jnp.max(cm, axis=-1, keepdims=True))
        alpha = jnp.exp2(m_prev - m_new)
        p = jnp.concatenate([jnp.exp2((sc - m_new).astype(BF16)) for sc in cols], axis=1)
        acc_sc[...] = alpha * acc_sc[...] + jnp.dot(p, vs, preferred_element_type=F32)
        m_sc[...] = m_new

    s_sc[...] = scores(0)

    def trip(i, carry):
        s_cur = s_sc[...]
        for u in range(unroll):
            j = i * unroll + u
            s_next = scores(jnp.minimum(j + 1, nk - 1))
            accumulate(s_cur, j)
            s_cur = s_next
        s_sc[...] = s_cur
        return carry

    lax.fori_loop(0, nk // unroll, trip, 0)
    acc = acc_sc[...]
    o_ref[0] = acc / acc[:, MLA_V:MLA_V + 1]


def _flash(q, k, v):
    b, s, hw = q.shape
    nh = hw // HEAD_PAD
    tq = min(T_Q, s)
    tk = min(T_K, s)
    nk = s // tk
    unroll = 4 if nk % 4 == 0 else (2 if nk % 2 == 0 else 1)
    return pl.pallas_call(
        functools.partial(_flash_kernel, tk=tk, nk=nk, unroll=unroll),
        grid=(b, nh, s // tq),
        in_specs=[pl.BlockSpec((1, tq, HEAD_PAD), lambda i, h, j: (i, j, h)),
                  pl.BlockSpec((1, s, HEAD_PAD), lambda i, h, j: (i, 0, h)),
                  pl.BlockSpec((1, s, HEAD_PAD), lambda i, h, j: (i, 0, h))],
        out_specs=pl.BlockSpec((1, tq, HEAD_PAD), lambda i, h, j: (i, j, h)),
        out_shape=jax.ShapeDtypeStruct((b, s, hw), F32),
        scratch_shapes=[pltpu.VMEM((tq, HEAD_PAD), F32), pltpu.VMEM((tq, HEAD_PAD), F32),
                        pltpu.VMEM((tq, tk), F32)],
        compiler_params=_cparams(("arbitrary", "arbitrary", "arbitrary")),
        name="flash",
    )(q, k, v)


def _post_kernel(na_ref, mla_ref, x_ref, ada_ref, gna_ref, gmla_ref, wona_ref, womla_ref,
                 lnpost_ref, lnpre_ref, wr_ref, br_ref,
                 x1_ref, h2_ref, e_ref, g_ref, rank_ref, cnt_ref, base_sc):
    first = jnp.logical_and(pl.program_id(0) == 0, pl.program_id(1) == 0)

    @pl.when(first)
    def _():
        base_sc[...] = jnp.zeros(base_sc.shape, F32)

    t = x_ref.shape[1]
    na = na_ref[0]
    nan_ = _rms(na) * gna_ref[...]
    ml = mla_ref[0]
    lane_w = lax.broadcasted_iota(jnp.int32, ml.shape, 1)
    ml = jnp.where((lane_w % HEAD_PAD) < MLA_V, ml, 0.0)
    mln = _rms(ml, n=MLA_HEADS * MLA_V) * gmla_ref[...]
    mix = (jnp.dot(nan_.astype(BF16), wona_ref[...], preferred_element_type=F32)
           + jnp.dot(mln.astype(BF16), womla_ref[...], preferred_element_type=F32))
    g_a = ada_ref[0, 2:3, :]
    sh_m = ada_ref[0, 3:4, :]
    sc_m = ada_ref[0, 4:5, :]
    x1 = x_ref[0] + g_a * (_rms(mix) * lnpost_ref[...])
    x1_ref[0] = x1
    h2 = (_rms(x1) * lnpre_ref[...]) * (1.0 + sc_m) + sh_m
    _store_token_tiles(h2_ref.at[0], h2)

    logits = jnp.dot(h2, wr_ref[...], preferred_element_type=F32,
                     precision=lax.Precision.HIGHEST) + br_ref[...]
    lane = lax.broadcasted_iota(jnp.int32, (t, LANE), 1)
    lane_f = lane.astype(F32)
    work = jnp.where(lane < N_EXPERTS, logits, -jnp.inf)
    vals, idxs = [], []
    onehot = jnp.zeros((t, LANE), F32)
    for _ in range(TOP_K):
        mk = jnp.max(work, axis=-1, keepdims=True)
        ik = jnp.min(jnp.where(work == mk, lane_f, float(LANE)), axis=-1, keepdims=True)
        sel = lane_f == ik
        work = jnp.where(sel, -jnp.inf, work)
        onehot = jnp.where(sel, 1.0, onehot)
        vals.append(mk)
        idxs.append(ik)
    exps = [jnp.exp(vk - vals[0]) for vk in vals]
    denom = exps[0]
    for ek in exps[1:]:
        denom = denom + ek

    r_i = lax.broadcasted_iota(jnp.int32, (t, t), 0)
    c_i = lax.broadcasted_iota(jnp.int32, (t, t), 1)
    tril = jnp.where(c_i < r_i, 1.0, 0.0).astype(BF16)
    pref = jnp.dot(tril, onehot.astype(BF16), preferred_element_type=F32) + base_sc[...]

    e_out = jnp.zeros((t, LANE), F32)
    g_out = jnp.zeros((t, LANE), F32)
    r_out = jnp.zeros((t, LANE), F32)
    for kk in range(TOP_K):
        rk = jnp.sum(jnp.where(lane_f == idxs[kk], pref, 0.0), axis=-1, keepdims=True)
        here = lane == kk
        e_out = jnp.where(here, idxs[kk], e_out)
        g_out = jnp.where(here, exps[kk] / denom, g_out)
        r_out = jnp.where(here, rk, r_out)
    e_ref[0] = e_out.astype(jnp.int32)
    g_ref[0] = g_out
    rank_ref[0] = r_out.astype(jnp.int32)
    base_sc[...] = base_sc[...] + jnp.sum(onehot, axis=0, keepdims=True)
    cnt_ref[...] = jnp.broadcast_to(base_sc[...], cnt_ref.shape)


def _post(na_out, mla_out, x, ada3, p):
    b, s, d = x.shape
    t = min(T_POST, s)
    full = lambda a: pl.BlockSpec(a.shape, lambda i, j: (0,) * a.ndim)
    tok = lambda w: pl.BlockSpec((1, t, w), lambda i, j: (i, j, 0))
    return pl.pallas_call(
        _post_kernel,
        grid=(b, s // t),
        in_specs=[tok(na_out.shape[-1]), tok(mla_out.shape[-1]), tok(d),
                  pl.BlockSpec((1, 6, d), lambda i, j: (i, 0, 0)),
                  full(p['g_na_out']), full(p['g_mla_pad']), full(p['wo_na']), full(p['wo_mla']),
                  full(p['ln_post_mix']), full(p['ln_pre_moe']), full(p['w_router']), full(p['b_router'])],
        out_specs=[tok(d), pl.BlockSpec((1, t * (d // LANE), LANE), lambda i, j: (i, j, 0)),
                   tok(LANE), tok(LANE), tok(LANE),
                   pl.BlockSpec((8, LANE), lambda i, j: (0, 0))],
        out_shape=[jax.ShapeDtypeStruct((b, s, d), F32), jax.ShapeDtypeStruct((b, s * (d // LANE), LANE), F32),
                   jax.ShapeDtypeStruct((b, s, LANE), jnp.int32), jax.ShapeDtypeStruct((b, s, LANE), F32),
                   jax.ShapeDtypeStruct((b, s, LANE), jnp.int32), jax.ShapeDtypeStruct((8, LANE), F32)],
        scratch_shapes=[pltpu.VMEM((1, LANE), F32)],
        compiler_params=_cparams(("arbitrary", "arbitrary")),
        name="post",
    )(na_out, mla_out, x, ada3, p['g_na_out'], p['g_mla_pad'], p['wo_na'], p['wo_mla'],
      p['ln_post_mix'], p['ln_pre_moe'], p['w_router'], p['b_router'])


def _experts_kernel(be_ref, nact_ref, tok_hbm, h_hbm, w1g_ref, w1l_ref, w2_ref, b1g_ref, b1l_ref, b2_ref,
                    y_ref, idx_sm, xbuf, isem, rsem, *, blk, nc):
    i = pl.program_id(0)
    nact = nact_ref[0]

    def idx_copy(step, slot):
        return pltpu.make_async_copy(tok_hbm.at[step], idx_sm.at[slot], isem.at[slot])

    def issue_rows(slot):
        def one(j, carry):
            src = pl.multiple_of(idx_sm[slot, j], nc)
            dst = pl.multiple_of(j * nc, nc)
            pltpu.make_async_copy(h_hbm.at[pl.ds(src, nc), :], xbuf.at[slot, pl.ds(dst, nc), :],
                                  rsem.at[slot]).start()
            return carry
        lax.fori_loop(0, blk, one, 0, unroll=16)

    def wait_rows(slot):
        pltpu.make_async_copy(h_hbm.at[pl.ds(0, blk * nc), :], xbuf.at[slot], rsem.at[slot]).wait()

    @pl.when(jnp.logical_and(i == 0, nact > 0))
    def _():
        idx_copy(0, 0).start()
        idx_copy(0, 0).wait()
        issue_rows(0)

        @pl.when(nact > 1)
        def _():
            idx_copy(1, 1).start()

    cur = lax.rem(i, 2)
    nxt = 1 - cur

    has_next = i + 1 < nact

    @pl.when(has_next)
    def _():
        idx_copy(i + 1, nxt).wait()

    @pl.when(i < nact)
    def _():
        wait_rows(cur)
        for j in range(blk):
            @pl.when(has_next)
            def _():
                src = pl.multiple_of(idx_sm[nxt, j], nc)
                pltpu.make_async_copy(h_hbm.at[pl.ds(src, nc), :], xbuf.at[nxt, pl.ds(j * nc, nc), :],
                                      rsem.at[nxt]).start()
        x = _load_token_tiles(xbuf.at[cur], nc).astype(BF16)
        ug = jnp.dot(x, w1g_ref[0], preferred_element_type=F32) + b1g_ref[0]
        ul = jnp.dot(x, w1l_ref[0], preferred_element_type=F32) + b1l_ref[0]
        x_glu = jnp.minimum(ug, SWIGLU_LIMIT)
        x_lin = jnp.clip(ul, -SWIGLU_LIMIT, SWIGLU_LIMIT)
        act = (x_lin + 1.0) * (x_glu * (1.0 / (1.0 + jnp.exp(-SWIGLU_ALPHA * x_glu))))
        y = jnp.dot(act.astype(BF16), w2_ref[0], preferred_element_type=F32) + b2_ref[0]
        _store_token_tiles(y_ref, y)

    @pl.when(i + 2 < nact)
    def _():
        idx_copy(i + 2, cur).start()

    @pl.when(i >= nact)
    def _():
        y_ref[...] = jnp.zeros(y_ref.shape, F32)


def _experts(h2, slot_tok, block_e, nact, p, nc):
    nb, blk = slot_tok.shape
    wspec = lambda a: pl.BlockSpec((1,) + a.shape[1:], lambda i, be, na: (be[i], 0, 0))
    grid_spec = pltpu.PrefetchScalarGridSpec(
        num_scalar_prefetch=2,
        grid=(nb,),
        in_specs=[pl.BlockSpec(memory_space=pl.ANY), pl.BlockSpec(memory_space=pl.ANY),
                  wspec(p['w1g']), wspec(p['w1l']), wspec(p['w2']),
                  wspec(p['b1g']), wspec(p['b1l']), wspec(p['b2'])],
        out_specs=pl.BlockSpec((blk * nc, LANE), lambda i, be, na: (i, 0)),
        scratch_shapes=[pltpu.SMEM((2, blk), jnp.int32), pltpu.VMEM((2, blk * nc, LANE), F32),
                        pltpu.SemaphoreType.DMA((2,)), pltpu.SemaphoreType.DMA((2,))],
    )
    return pl.pallas_call(
        functools.partial(_experts_kernel, blk=blk, nc=nc),
        grid_spec=grid_spec,
        out_shape=jax.ShapeDtypeStruct((nb * blk * nc, LANE), F32),
        compiler_params=_cparams(("arbitrary",)),
        name="experts",
    )(block_e, nact, slot_tok, h2, p['w1g'], p['w1l'], p['w2'], p['b1g'], p['b1l'], p['b2'])


def _deint_kernel(w_ref, pg_ref, pl_ref, g_ref, l_ref):
    w = w_ref[0].astype(BF16)
    g_ref[0] = jnp.dot(w, pg_ref[...], preferred_element_type=F32).astype(BF16)
    l_ref[0] = jnp.dot(w, pl_ref[...], preferred_element_type=F32).astype(BF16)


def _deinterleave_w1(w_mlp1):
    ne, d, f2 = w_mlp1.shape
    f = f2 // 2
    rows = min(512, d)
    src = lax.broadcasted_iota(jnp.int32, (f2, f), 0)
    dst = lax.broadcasted_iota(jnp.int32, (f2, f), 1)
    pg = (src == 2 * dst).astype(BF16)
    pl_ = (src == 2 * dst + 1).astype(BF16)
    return pl.pallas_call(
        _deint_kernel,
        grid=(ne, d // rows),
        in_specs=[pl.BlockSpec((1, rows, f2), lambda e, r: (e, r, 0)),
                  pl.BlockSpec((f2, f), lambda e, r: (0, 0)),
                  pl.BlockSpec((f2, f), lambda e, r: (0, 0))],
        out_specs=[pl.BlockSpec((1, rows, f), lambda e, r: (e, r, 0))] * 2,
        out_shape=[jax.ShapeDtypeStruct((ne, d, f), BF16)] * 2,
        compiler_params=_cparams(("arbitrary", "arbitrary")),
        name="deint",
    )(w_mlp1, pg, pl_)


def _combine_kernel(dest_hbm, y_hbm, x1_ref, g_ref, ada_ref, ln_ref, o_ref,
                    idx_sm, ybuf, isem, rsem, *, t, nsteps, nc):
    i = pl.program_id(0) * pl.num_programs(1) + pl.program_id(1)

    def idx_copy(step, slot):
        return pltpu.make_async_copy(dest_hbm.at[step], idx_sm.at[slot], isem.at[slot])

    def issue_rows(slot):
        def one(j, carry):
            for kk in range(TOP_K):
                src = pl.multiple_of(idx_sm[slot, j * TOP_K + kk], nc)
                dst = pl.multiple_of(j * nc, nc)
                pltpu.make_async_copy(y_hbm.at[pl.ds(src, nc), :], ybuf.at[slot, kk, pl.ds(dst, nc), :],
                                      rsem.at[slot]).start()
            return carry
        lax.fori_loop(0, t, one, 0, unroll=4)

    def wait_rows(slot):
        for kk in range(TOP_K):
            pltpu.make_async_copy(y_hbm.at[pl.ds(0, t * nc), :], ybuf.at[slot, kk], rsem.at[slot]).wait()

    @pl.when(i == 0)
    def _():
        idx_copy(0, 0).start()
        idx_copy(0, 0).wait()
        issue_rows(0)
        if nsteps > 1:
            idx_copy(1, 1).start()

    cur = lax.rem(i, 2)
    nxt = 1 - cur

    has_next = i + 1 < nsteps

    @pl.when(has_next)
    def _():
        idx_copy(i + 1, nxt).wait()

    for j in range(t):
        for kk in range(TOP_K):
            @pl.when(has_next)
            def _():
                src = pl.multiple_of(idx_sm[nxt, j * TOP_K + kk], nc)
                pltpu.make_async_copy(y_hbm.at[pl.ds(src, nc), :], ybuf.at[nxt, kk, pl.ds(j * nc, nc), :],
                                      rsem.at[nxt]).start()

    @pl.when(i + 2 < nsteps)
    def _():
        idx_copy(i + 2, cur).start()

    wait_rows(cur)
    g = g_ref[0]
    y = jnp.zeros((t, x1_ref.shape[2]), F32)
    for kk in range(TOP_K):
        y = y + _load_token_tiles(ybuf.at[cur, kk], nc) * g[:, kk:kk + 1]
    g_m = ada_ref[0, 5:6, :]
    o_ref[0] = x1_ref[0] + g_m * (_rms(y) * ln_ref[...])


def _combine(dest, y_sorted, x1, gates, ada3, ln_post_moe):
    b, s, d = x1.shape
    t = min(T_COMB, s)
    ns = s // t
    nsteps = b * ns
    nc = d // LANE
    dest2 = (dest * nc).reshape(nsteps, t * TOP_K)
    return pl.pallas_call(
        functools.partial(_combine_kernel, t=t, nsteps=nsteps, nc=nc),
        grid=(b, ns),
        in_specs=[pl.BlockSpec(memory_space=pl.ANY), pl.BlockSpec(memory_space=pl.ANY),
                  pl.BlockSpec((1, t, d), lambda i, j: (i, j, 0)),
                  pl.BlockSpec((1, t, LANE), lambda i, j: (i, j, 0)),
                  pl.BlockSpec((1, 6, d), lambda i, j: (i, 0, 0)),
                  pl.BlockSpec(ln_post_moe.shape, lambda i, j: (0, 0))],
        out_specs=pl.BlockSpec((1, t, d), lambda i, j: (i, j, 0)),
        out_shape=jax.ShapeDtypeStruct((b, s, d), F32),
        scratch_shapes=[pltpu.SMEM((2, t * TOP_K), jnp.int32),
                        pltpu.VMEM((2, TOP_K, t * nc, LANE), F32),
                        pltpu.SemaphoreType.DMA((2,)), pltpu.SemaphoreType.DMA((2,))],
        compiler_params=_cparams(("arbitrary", "arbitrary")),
        name="combine",
    )(dest2, y_sorted, x1, gates, ada3, ln_post_moe)


def _rope_partner(w):
    half = MLA_ROPE // 2
    return jnp.concatenate([-w[..., half:], w[..., :half]], axis=-1)


def _prep_params(ln_pre_mix, ln_post_mix, ln_pre_moe, ln_post_moe, w_in, na_rpb, q_a_norm, w_q_b,
                 kv_a_norm, w_kv_b, g_na_out, g_mla_out, w_o, w_router, b_router,
                 w_mlp1, b_mlp1, w_mlp2, b_mlp2):
    d = w_in.shape[0]
    na_w = NA_HEADS * HEAD_DIM
    q_lora = w_q_b.shape[0]
    kv_lora = w_kv_b.shape[0]
    qk = MLA_NOPE + MLA_ROPE
    main_w = 3 * na_w + q_lora + kv_lora
    w_kr = w_in[:, main_w:main_w + MLA_ROPE]
    zpad = lambda n: jnp.zeros((d, n), F32)
    kra = jnp.concatenate([zpad(MLA_NOPE), w_kr, zpad(HEAD_PAD - qk)], axis=1)
    krb = jnp.concatenate([zpad(MLA_NOPE), _rope_partner(w_kr), zpad(HEAD_PAD - qk)], axis=1)
    w1 = jnp.concatenate([w_in[:, :main_w], kra, krb], axis=1).astype(BF16)

    wq = w_q_b.reshape(q_lora, MLA_HEADS, qk)
    zq = lambda n: jnp.zeros((q_lora, MLA_HEADS, n), F32)
    wqa = jnp.concatenate([wq, zq(HEAD_PAD - qk)], axis=2).reshape(q_lora, -1).astype(BF16)
    wqb = jnp.concatenate([zq(MLA_NOPE), _rope_partner(wq[:, :, MLA_NOPE:]), zq(HEAD_PAD - qk)],
                          axis=2).reshape(q_lora, -1).astype(BF16)
    wkv = w_kv_b.reshape(kv_lora, MLA_HEADS, MLA_NOPE + MLA_V)
    zk = lambda n: jnp.zeros((kv_lora, MLA_HEADS, n), F32)
    wk = jnp.concatenate([wkv[:, :, :MLA_NOPE], zk(HEAD_PAD - MLA_NOPE)], axis=2).reshape(kv_lora, -1).astype(BF16)
    wv = jnp.concatenate([wkv[:, :, MLA_NOPE:], zk(HEAD_PAD - MLA_V)], axis=2).reshape(kv_lora, -1).astype(BF16)

    mla_w = MLA_HEADS * MLA_V
    g_mla_pad = jnp.concatenate([g_mla_out.reshape(MLA_HEADS, MLA_V),
                                 jnp.zeros((MLA_HEADS, HEAD_PAD - MLA_V), F32)], axis=1).reshape(1, -1)
    wo_mla = w_o[na_w:na_w + mla_w].reshape(MLA_HEADS, MLA_V, d)
    wo_mla = jnp.concatenate([wo_mla, jnp.zeros((MLA_HEADS, HEAD_PAD - MLA_V, d), F32)],
                             axis=1).reshape(MLA_HEADS * HEAD_PAD, d).astype(BF16)
    w_r = jnp.concatenate([w_router, jnp.zeros((d, LANE - N_EXPERTS), F32)], axis=1)
    b_r = jnp.concatenate([b_router, jnp.zeros((LANE - N_EXPERTS,), F32)]).reshape(1, LANE)
    ne = w_mlp1.shape[0]
    w1g, w1l = _deinterleave_w1(w_mlp1)
    return {
        'ln_pre_mix': ln_pre_mix.reshape(1, d), 'ln_post_mix': ln_post_mix.reshape(1, d),
        'ln_pre_moe': ln_pre_moe.reshape(1, d), 'ln_post_moe': ln_post_moe.reshape(1, d),
        'w1': w1, 'q_a_norm': q_a_norm.reshape(1, q_lora), 'wqa': wqa, 'wqb': wqb,
        'kv_a_norm': kv_a_norm.reshape(1, kv_lora), 'wk': wk, 'wv': wv,
        'na_bias': _na_bias_table(na_rpb),
        'g_na_out': g_na_out.reshape(1, na_w), 'g_mla_pad': g_mla_pad,
        'wo_na': w_o[:na_w].astype(BF16), 'wo_mla': wo_mla,
        'w_router': w_r, 'b_router': b_r,
        'w1g': w1g, 'w1l': w1l,
        'w2': w_mlp2.astype(BF16),
        'b1g': b_mlp1[:, 0::2].reshape(ne, 1, -1), 'b1l': b_mlp1[:, 1::2].reshape(ne, 1, -1),
        'b2': b_mlp2.reshape(ne, 1, -1),
    }


def _rope_lane_tables(s):
    half = MLA_ROPE // 2
    inv = ROPE_THETA ** (-jnp.arange(half, dtype=F32) / half)
    ang = jnp.arange(s, dtype=F32)[:, None] * inv[None, :]
    cos, sin = jnp.cos(ang), jnp.sin(ang)
    z = lambda n: jnp.zeros((s, n), F32)
    tail = HEAD_PAD - MLA_NOPE - MLA_ROPE
    ck = jnp.concatenate([z(MLA_NOPE), cos, cos, z(tail)], axis=1)
    sk = jnp.concatenate([z(MLA_NOPE), sin, sin, z(tail)], axis=1)
    return ck, sk


def _moe_plan(e4, rank4, counts, n):
    nk = n * TOP_K
    padded = (counts + MOE_BLK - 1) // MOE_BLK * MOE_BLK
    pad_end = jnp.cumsum(padded)
    pad_start = pad_end - padded
    dest = pad_start[e4] + rank4
    nb = (nk + N_EXPERTS * (MOE_BLK - 1) + MOE_BLK - 1) // MOE_BLK
    tok = jnp.broadcast_to(jnp.arange(n, dtype=jnp.int32)[:, None], (n, TOP_K))
    slot_tok = jnp.zeros((nb * MOE_BLK,), jnp.int32).at[dest.reshape(-1)].set(tok.reshape(-1))
    block_start = jnp.arange(nb, dtype=jnp.int32) * MOE_BLK
    block_e = jnp.minimum(jnp.searchsorted(pad_end, block_start, side='right'),
                          N_EXPERTS - 1).astype(jnp.int32)
    nact = (pad_end[-1] // MOE_BLK).astype(jnp.int32).reshape(1)
    return dest.astype(jnp.int32), slot_tok.reshape(nb, MOE_BLK), block_e, nact


def _encoder_layer(x, ada3, p):
    b, s, d = x.shape
    ck, sk = _rope_lane_tables(s)
    qna, kna, vna, q, k, v = _inproj(x, ada3, p, ck, sk)
    na_out = _na(qna, kna, vna, p['na_bias'])
    mla_out = _flash(q, k, v)
    x1, h2, e, g, rank, cnt = _post(na_out, mla_out, x, ada3, p)
    n = b * s
    e4 = e.reshape(n, LANE)[:, :TOP_K]
    rank4 = rank.reshape(n, LANE)[:, :TOP_K]
    counts = cnt[0, :N_EXPERTS].astype(jnp.int32)
    dest, slot_tok, block_e, nact = _moe_plan(e4, rank4, counts, n)
    nc = d // LANE
    y_sorted = _experts(h2.reshape(n * nc, LANE), slot_tok * nc, block_e, nact, p, nc)
    return _combine(dest, y_sorted, x1, g, ada3, p['ln_post_moe'])


def kernel(x_prompt, x_sample, c_prompt, c_sample, ln_pre_mix, ln_post_mix, ln_pre_moe, ln_post_moe, w_ada, b_ada, w_in, na_rpb, q_a_norm, w_q_b, kv_a_norm, w_kv_b, g_na_out, g_mla_out, w_o, w_router, b_router, w_mlp1, b_mlp1, w_mlp2, b_mlp2):
    depth = w_ada.shape[0]
    d = x_prompt.shape[-1]
    bp, bs = x_prompt.shape[0], x_sample.shape[0]
    rows = -(-(bp + bs) // 8) * 8
    y_prompt, y_sample = x_prompt, x_sample
    for l in range(depth):
        p = _prep_params(ln_pre_mix[l], ln_post_mix[l], ln_pre_moe[l], ln_post_moe[l], w_in[l], na_rpb[l],
                         q_a_norm[l], w_q_b[l], kv_a_norm[l], w_kv_b[l], g_na_out[l], g_mla_out[l], w_o[l],
                         w_router[l], b_router[l], w_mlp1[l], b_mlp1[l], w_mlp2[l], b_mlp2[l])
        c_all = jnp.concatenate([c_prompt, c_sample, jnp.zeros((rows - bp - bs, d), F32)], axis=0)
        ada3 = _ada(c_all, w_ada[l], b_ada[l]).reshape(rows, 6, d)
        y_prompt = _encoder_layer(y_prompt, ada3[:bp], p)
        y_sample = _encoder_layer(y_sample, ada3[bp:bp + bs], p)
    return (y_prompt, y_sample)
```

```python
import functools

import numpy as np
import jax
import jax.numpy as jnp
from jax import lax
from jax.experimental import pallas as pl
from jax.experimental.pallas import tpu as pltpu

F32 = jnp.float32
BF16 = jnp.bfloat16

GRID_W = 64
HEAD_DIM = 64
NA_HEADS = 8
NA_KR = 8
NA_KC = 16
MLA_HEADS = 8
MLA_NOPE = 64
MLA_ROPE = 32
MLA_V = 64
ROPE_THETA = 10000.0
N_EXPERTS = 32
TOP_K = 4
SWIGLU_LIMIT = 7.0
SWIGLU_ALPHA = 1.702
EPS = 1e-6

LANE = 128
HEAD_PAD = 128
NEG_BIG = -1e30
LOG2E = 1.4426950408889634

T_PROJ = 256
T_Q = 512
T_K = 1024
NA_ROWS = 8
T_POST = 256
MOE_BLK = 512
T_COMB = 128
VMEM_LIMIT = 56 * 1024 * 1024


def _rms(x, n=None):
    n = x.shape[-1] if n is None else n
    return x * lax.rsqrt(jnp.sum(x * x, axis=-1, keepdims=True) * (1.0 / n) + EPS)


def _cparams(sem):
    return pltpu.CompilerParams(dimension_semantics=sem, vmem_limit_bytes=VMEM_LIMIT)


def _store_token_tiles(ref, x):
    rows, d = x.shape
    nc = d // LANE
    for c in range(nc):
        ref[pl.ds(c, rows, stride=nc), :] = x[:, c * LANE:(c + 1) * LANE]


def _load_token_tiles(ref, nc):
    rows = ref.shape[0] // nc
    return jnp.concatenate([ref[pl.ds(c, rows, stride=nc), :] for c in range(nc)], axis=1)


def _ada_kernel(c_ref, w_ref, b_ref, o_ref):
    c = c_ref[...]
    s = c / (1.0 + jnp.exp(-c))
    o_ref[...] = jnp.dot(s, w_ref[...], preferred_element_type=F32,
                         precision=lax.Precision.HIGHEST) + b_ref[...]


def _ada(c_all, w_ada, b_ada):
    rows, d = c_all.shape
    n_out = w_ada.shape[1]
    return pl.pallas_call(
        _ada_kernel,
        grid=(n_out // d,),
        in_specs=[pl.BlockSpec((rows, d), lambda j: (0, 0)),
                  pl.BlockSpec((d, d), lambda j: (0, j)),
                  pl.BlockSpec((1, d), lambda j: (0, j))],
        out_specs=pl.BlockSpec((rows, d), lambda j: (0, j)),
        out_shape=jax.ShapeDtypeStruct((rows, n_out), F32),
        compiler_params=_cparams(("arbitrary",)),
        name="ada",
    )(c_all, w_ada, b_ada.reshape(1, n_out))


def _inproj_kernel(x_ref, ada_ref, g_ref, ck_ref, sk_ref, w1_ref, qan_ref, wqa_ref, wqb_ref,
                   kvan_ref, wk_ref, wv_ref,
                   qna_ref, kna_ref, vna_ref, q_ref, k_ref, v_ref, *, na_w, q_lora, kv_lora, qscale):
    x = x_ref[0]
    sh = ada_ref[0, 0:1, :]
    sc = ada_ref[0, 1:2, :]
    h = (_rms(x) * g_ref[...]) * (1.0 + sc) + sh
    proj = jnp.dot(h.astype(BF16), w1_ref[...], preferred_element_type=F32)
    qna_ref[0] = (proj[:, 0:na_w] * (1.0 / float(np.sqrt(HEAD_DIM)))).astype(BF16)
    kna_ref[0] = proj[:, na_w:2 * na_w].astype(BF16)
    vna_ref[0] = proj[:, 2 * na_w:3 * na_w].astype(BF16)
    o = 3 * na_w
    cqn = (_rms(proj[:, o:o + q_lora]) * qan_ref[...]).astype(BF16)
    o += q_lora
    ckvn = (_rms(proj[:, o:o + kv_lora]) * kvan_ref[...]).astype(BF16)
    o += kv_lora
    kra = proj[:, o:o + HEAD_PAD]
    krb = proj[:, o + HEAD_PAD:o + 2 * HEAD_PAD]
    qa = jnp.dot(cqn, wqa_ref[...], preferred_element_type=F32)
    qb = jnp.dot(cqn, wqb_ref[...], preferred_element_type=F32)
    kn = jnp.dot(ckvn, wk_ref[...], preferred_element_type=F32)
    vv = jnp.dot(ckvn, wv_ref[...], preferred_element_type=F32)
    ck = ck_ref[...]
    sk = sk_ref[...]
    lane = lax.broadcasted_iota(jnp.int32, (1, HEAD_PAD), 1)
    nope = (lane < MLA_NOPE).astype(F32)
    ones_col = (lane == MLA_V).astype(F32)
    cq = (nope + ck) * qscale
    sq = sk * qscale
    kpe = kra * ck + krb * sk
    for hd in range(MLA_HEADS):
        sl = slice(hd * HEAD_PAD, (hd + 1) * HEAD_PAD)
        q_ref[0, sl, :] = (qa[:, sl] * cq + qb[:, sl] * sq).T.astype(BF16)
        k_ref[0, :, sl] = (kn[:, sl] + kpe).astype(BF16)
        v_ref[0, sl, :] = (vv[:, sl] + ones_col).T.astype(BF16)


def _inproj(x, ada3, p, ck, sk):
    b, s, d = x.shape
    t = min(T_PROJ, s)
    na_w = NA_HEADS * HEAD_DIM
    hw = MLA_HEADS * HEAD_PAD
    q_lora = p['wqa'].shape[0]
    kv_lora = p['wk'].shape[0]
    w1 = p['w1']
    kern = functools.partial(_inproj_kernel, na_w=na_w, q_lora=q_lora, kv_lora=kv_lora,
                             qscale=LOG2E / float(np.sqrt(MLA_NOPE + MLA_ROPE)))
    full = lambda a: pl.BlockSpec(a.shape, lambda i, j: (0,) * a.ndim)
    tok = lambda w: pl.BlockSpec((1, t, w), lambda i, j: (i, j, 0))
    tok_t = pl.BlockSpec((1, hw, t), lambda i, j: (i, 0, j))
    outs = pl.pallas_call(
        kern,
        grid=(b, s // t),
        in_specs=[tok(d),
                  pl.BlockSpec((1, 6, d), lambda i, j: (i, 0, 0)),
                  full(p['ln_pre_mix']),
                  pl.BlockSpec((t, HEAD_PAD), lambda i, j: (j, 0)),
                  pl.BlockSpec((t, HEAD_PAD), lambda i, j: (j, 0)),
                  full(w1), full(p['q_a_norm']), full(p['wqa']), full(p['wqb']),
                  full(p['kv_a_norm']), full(p['wk']), full(p['wv'])],
        out_specs=[tok(na_w), tok(na_w), tok(na_w), tok_t, tok(hw), tok_t],
        out_shape=[jax.ShapeDtypeStruct((b, s, na_w), BF16)] * 3
                  + [jax.ShapeDtypeStruct((b, hw, s), BF16), jax.ShapeDtypeStruct((b, s, hw), BF16),
                     jax.ShapeDtypeStruct((b, hw, s), BF16)],
        compiler_params=_cparams(("arbitrary", "arbitrary")),
        name="inproj",
    )(x, ada3, p['ln_pre_mix'], ck, sk, w1, p['q_a_norm'], p['wqa'], p['wqb'],
      p['kv_a_norm'], p['wk'], p['wv'])
    return outs


def _na_kernel(q_ref, kp_ref, kc_ref, kn_ref, vp_ref, vc_ref, vn_ref, bias_ref, o_ref,
               kcat, vcat, *, rows):
    blk = pl.program_id(1)
    nb = NA_ROWS * GRID_W
    win = NA_KR * GRID_W
    for i, (kr, vr) in enumerate(((kp_ref, vp_ref), (kc_ref, vc_ref), (kn_ref, vn_ref))):
        kcat[i * nb:(i + 1) * nb, :] = kr[0]
        vcat[i * nb:(i + 1) * nb, :] = vr[0]
    lane = lax.broadcasted_iota(jnp.int32, (GRID_W, LANE), 1)
    lo = lane < HEAD_DIM

    def row_body(rl, carry):
        r = blk * NA_ROWS + rl
        rs = jnp.clip(r - NA_KR // 2, 0, rows - NA_KR)
        d = r - rs
        off = pl.multiple_of((rs - (blk - 1) * NA_ROWS) * GRID_W, GRID_W)
        qoff = pl.multiple_of(rl * GRID_W, GRID_W)
        scores = []
        for hp in range(NA_HEADS // 2):
            sl = slice(hp * LANE, (hp + 1) * LANE)
            qp = q_ref[0, pl.ds(qoff, GRID_W), sl]
            kw = kcat[pl.ds(off, win), sl]
            for hh in range(2):
                qm = jnp.where(lo if hh == 0 else jnp.logical_not(lo), qp, jnp.zeros_like(qp))
                s = lax.dot_general(qm, kw, (((1,), (1,)), ((), ())), preferred_element_type=F32)
                scores.append(s + bias_ref[d, hp * 2 + hh])
        probs, inv_l = [], []
        for s in scores:
            m = jnp.max(s, axis=-1, keepdims=True)
            e = jnp.exp(s - m)
            inv_l.append(1.0 / jnp.sum(e, axis=-1, keepdims=True))
            probs.append(e.astype(BF16))
        pair_outs = []
        for hp in range(NA_HEADS // 2):
            vw = vcat[pl.ds(off, win), hp * LANE:(hp + 1) * LANE]
            outs = [jnp.dot(probs[hp * 2 + hh], vw, preferred_element_type=F32) * inv_l[hp * 2 + hh]
                    for hh in range(2)]
            pair_outs.append(jnp.where(lo, outs[0], outs[1]))
        o_ref[0, pl.ds(qoff, GRID_W), :] = jnp.concatenate(pair_outs, axis=1)
        return carry

    lax.fori_loop(0, NA_ROWS, row_body, 0)


def _na_bias_table(rpb):
    nh, nrow, nrel = rpb.shape
    cols = np.arange(GRID_W)
    cstart = np.clip(cols - NA_KC // 2, 0, GRID_W - NA_KC)
    j = np.arange(GRID_W)
    inwin = (j[None, :] >= cstart[:, None]) & (j[None, :] < cstart[:, None] + NA_KC)
    period = 2 * GRID_W - 1
    vpad = jnp.concatenate([rpb[..., NA_KC - 1:], jnp.zeros((nh, nrow, period - nrel), rpb.dtype),
                            rpb[..., :NA_KC - 1]], axis=-1)
    toep = jnp.tile(vpad, (1, 1, GRID_W))[..., :GRID_W * (period - 1)]
    toep = toep.reshape(nh, nrow, GRID_W, period - 1)[..., :GRID_W]
    tabs = []
    for d in range(NA_KR):
        t = toep[:, NA_KR - 1 - d:2 * NA_KR - 1 - d]
        tabs.append(t.transpose(0, 2, 1, 3))
    tab = jnp.stack(tabs, axis=0).astype(F32)
    tab = jnp.where(jnp.asarray(inwin)[None, None, :, None, :], tab, NEG_BIG)
    return tab.reshape(NA_KR, nh, GRID_W, NA_KR * GRID_W)


def _na(q, k, v, bias):
    b, s, w = q.shape
    rows = s // GRID_W
    assert rows % NA_ROWS == 0 and rows >= 2 * NA_ROWS
    nblk = rows // NA_ROWS
    nb = NA_ROWS * GRID_W
    cur = pl.BlockSpec((1, nb, w), lambda i, j: (i, j, 0))
    prv = pl.BlockSpec((1, nb, w), lambda i, j: (i, jnp.maximum(j - 1, 0), 0))
    nxt = pl.BlockSpec((1, nb, w), lambda i, j: (i, jnp.minimum(j + 1, nblk - 1), 0))
    return pl.pallas_call(
        functools.partial(_na_kernel, rows=rows),
        grid=(b, nblk),
        in_specs=[cur, prv, cur, nxt, prv, cur, nxt,
                  pl.BlockSpec(bias.shape, lambda i, j: (0, 0, 0, 0))],
        out_specs=cur,
        out_shape=jax.ShapeDtypeStruct((b, s, w), F32),
        scratch_shapes=[pltpu.VMEM((3 * nb, w), BF16), pltpu.VMEM((3 * nb, w), BF16)],
        compiler_params=_cparams(("arbitrary", "arbitrary")),
        name="na",
    )(q, k, k, k, v, v, v, bias)


def _flash_kernel(qt_ref, k_ref, vt_ref, o_ref, m_sc, acc_sc, s_sc, *, tk, nk, unroll):
    qt = qt_ref[0]
    m_sc[...] = jnp.full(m_sc.shape, -jnp.inf, F32)
    acc_sc[...] = jnp.zeros(acc_sc.shape, F32)

    def scores(j):
        off = pl.multiple_of(j * tk, tk)
        ks = k_ref[0, pl.ds(off, tk), :]
        return jnp.dot(ks, qt, preferred_element_type=F32)

    def accumulate(st, j):
        off = pl.multiple_of(j * tk, tk)
        vt = vt_ref[0, :, pl.ds(off, tk)]
        m_prev = m_sc[...]
        m_new = jnp.maximum(m_prev, jnp.max(st, axis=0, keepdims=True))
        alpha = jnp.exp2(m_prev - m_new)
        pt = jnp.exp2((st - m_new).astype(BF16))
        acc_sc[...] = alpha * acc_sc[...] + jnp.dot(vt, pt, preferred_element_type=F32)
        m_sc[...] = m_new

    s_sc[...] = scores(0)

    def trip(i, carry):
        s_cur = s_sc[...]
        for u in range(unroll):
            j = i * unroll + u
            s_next = scores(jnp.minimum(j + 1, nk - 1))
            accumulate(s_cur, j)
            s_cur = s_next
        s_sc[...] = s_cur
        return carry

    lax.fori_loop(0, nk // unroll, trip, 0)
    acc = acc_sc[...]
    o_ref[0] = (acc / acc[MLA_V:MLA_V + 1, :]).T


def _flash(qt, k, vt):
    b, s, hw = k.shape
    nh = hw // HEAD_PAD
    tq = min(T_Q, s)
    tk = min(T_K, s)
    nk = s // tk
    unroll = 2 if nk % 2 == 0 else 1
    return pl.pallas_call(
        functools.partial(_flash_kernel, tk=tk, nk=nk, unroll=unroll),
        grid=(b, nh, s // tq),
        in_specs=[pl.BlockSpec((1, HEAD_PAD, tq), lambda i, h, j: (i, h, j)),
                  pl.BlockSpec((1, s, HEAD_PAD), lambda i, h, j: (i, 0, h)),
                  pl.BlockSpec((1, HEAD_PAD, s), lambda i, h, j: (i, h, 0))],
        out_specs=pl.BlockSpec((1, tq, HEAD_PAD), lambda i, h, j: (i, j, h)),
        out_shape=jax.ShapeDtypeStruct((b, s, hw), F32),
        scratch_shapes=[pltpu.VMEM((1, tq), F32), pltpu.VMEM((HEAD_PAD, tq), F32),
                        pltpu.VMEM((tk, tq), F32)],
        compiler_params=_cparams(("arbitrary", "arbitrary", "arbitrary")),
        name="flash",
    )(qt, k, vt)


def _post_kernel(na_ref, mla_ref, x_ref, ada_ref, gna_ref, gmla_ref, wona_ref, womla_ref,
                 lnpost_ref, lnpre_ref, wr_ref, br_ref,
                 x1_ref, h2_ref, e_ref, g_ref, rank_ref, cnt_ref, base_sc):
    first = jnp.logical_and(pl.program_id(0) == 0, pl.program_id(1) == 0)

    @pl.when(first)
    def _():
        base_sc[...] = jnp.zeros(base_sc.shape, F32)

    t = x_ref.shape[1]
    na = na_ref[0]
    nan_ = _rms(na) * gna_ref[...]
    ml = mla_ref[0]
    lane_w = lax.broadcasted_iota(jnp.int32, ml.shape, 1)
    ml = jnp.where((lane_w % HEAD_PAD) < MLA_V, ml, 0.0)
    mln = _rms(ml, n=MLA_HEADS * MLA_V) * gmla_ref[...]
    mix = (jnp.dot(nan_.astype(BF16), wona_ref[...], preferred_element_type=F32)
           + jnp.dot(mln.astype(BF16), womla_ref[...], preferred_element_type=F32))
    g_a = ada_ref[0, 2:3, :]
    sh_m = ada_ref[0, 3:4, :]
    sc_m = ada_ref[0, 4:5, :]
    x1 = x_ref[0] + g_a * (_rms(mix) * lnpost_ref[...])
    x1_ref[0] = x1
    h2 = (_rms(x1) * lnpre_ref[...]) * (1.0 + sc_m) + sh_m
    _store_token_tiles(h2_ref.at[0], h2)

    h_hi = h2.astype(BF16)
    h_lo = (h2 - h_hi.astype(F32)).astype(BF16)
    logits = jnp.dot(jnp.concatenate([h_hi, h_lo, h_hi], axis=1), wr_ref[...],
                     preferred_element_type=F32) + br_ref[...]
    lane = lax.broadcasted_iota(jnp.int32, (t, LANE), 1)
    lane_f = lane.astype(F32)
    work = jnp.where(lane < N_EXPERTS, logits, -jnp.inf)
    vals, idxs = [], []
    onehot = jnp.zeros((t, LANE), F32)
    for _ in range(TOP_K):
        mk = jnp.max(work, axis=-1, keepdims=True)
        ik = jnp.min(jnp.where(work == mk, lane_f, float(LANE)), axis=-1, keepdims=True)
        sel = lane_f == ik
        work = jnp.where(sel, -jnp.inf, work)
        onehot = jnp.where(sel, 1.0, onehot)
        vals.append(mk)
        idxs.append(ik)
    exps = [jnp.exp(vk - vals[0]) for vk in vals]
    denom = exps[0]
    for ek in exps[1:]:
        denom = denom + ek

    r_i = lax.broadcasted_iota(jnp.int32, (t, t), 0)
    c_i = lax.broadcasted_iota(jnp.int32, (t, t), 1)
    tril = jnp.where(c_i < r_i, 1.0, 0.0).astype(BF16)
    pref = jnp.dot(tril, onehot.astype(BF16), preferred_element_type=F32) + base_sc[...]

    e_out = jnp.zeros((t, LANE), F32)
    g_out = jnp.zeros((t, LANE), F32)
    r_out = jnp.zeros((t, LANE), F32)
    for kk in range(TOP_K):
        rk = jnp.sum(jnp.where(lane_f == idxs[kk], pref, 0.0), axis=-1, keepdims=True)
        here = lane == kk
        e_out = jnp.where(here, idxs[kk], e_out)
        g_out = jnp.where(here, exps[kk] / denom, g_out)
        r_out = jnp.where(here, rk, r_out)
    e_ref[0] = e_out.astype(jnp.int32)
    g_ref[0] = g_out
    rank_ref[0] = r_out.astype(jnp.int32)
    base_sc[...] = base_sc[...] + jnp.sum(onehot, axis=0, keepdims=True)
    cnt_ref[...] = jnp.broadcast_to(base_sc[...], cnt_ref.shape)


def _post(na_out, mla_out, x, ada3, p):
    b, s, d = x.shape
    t = min(T_POST, s)
    full = lambda a: pl.BlockSpec(a.shape, lambda i, j: (0,) * a.ndim)
    tok = lambda w: pl.BlockSpec((1, t, w), lambda i, j: (i, j, 0))
    return pl.pallas_call(
        _post_kernel,
        grid=(b, s // t),
        in_specs=[tok(na_out.shape[-1]), tok(mla_out.shape[-1]), tok(d),
                  pl.BlockSpec((1, 6, d), lambda i, j: (i, 0, 0)),
                  full(p['g_na_out']), full(p['g_mla_pad']), full(p['wo_na']), full(p['wo_mla']),
                  full(p['ln_post_mix']), full(p['ln_pre_moe']), full(p['w_router']), full(p['b_router'])],
        out_specs=[tok(d), pl.BlockSpec((1, t * (d // LANE), LANE), lambda i, j: (i, j, 0)),
                   tok(LANE), tok(LANE), tok(LANE),
                   pl.BlockSpec((8, LANE), lambda i, j: (0, 0))],
        out_shape=[jax.ShapeDtypeStruct((b, s, d), F32), jax.ShapeDtypeStruct((b, s * (d // LANE), LANE), F32),
                   jax.ShapeDtypeStruct((b, s, LANE), jnp.int32), jax.ShapeDtypeStruct((b, s, LANE), F32),
                   jax.ShapeDtypeStruct((b, s, LANE), jnp.int32), jax.ShapeDtypeStruct((8, LANE), F32)],
        scratch_shapes=[pltpu.VMEM((1, LANE), F32)],
        compiler_params=_cparams(("arbitrary", "arbitrary")),
        name="post",
    )(na_out, mla_out, x, ada3, p['g_na_out'], p['g_mla_pad'], p['wo_na'], p['wo_mla'],
      p['ln_post_mix'], p['ln_pre_moe'], p['w_router'], p['b_router'])


def _experts_kernel(be_ref, nact_ref, tok_hbm, h_hbm, w1g_ref, w1l_ref, w2_ref, b1g_ref, b1l_ref, b2_ref,
                    y_ref, idx_sm, xbuf, isem, rsem, *, blk, nc):
    i = pl.program_id(0)
    nact = nact_ref[0]

    def idx_copy(step, slot):
        return pltpu.make_async_copy(tok_hbm.at[step], idx_sm.at[slot], isem.at[slot])

    def issue_rows(slot):
        def one(j, carry):
            src = pl.multiple_of(idx_sm[slot, j], nc)
            dst = pl.multiple_of(j * nc, nc)
            pltpu.make_async_copy(h_hbm.at[pl.ds(src, nc), :], xbuf.at[slot, pl.ds(dst, nc), :],
                                  rsem.at[slot]).start()
            return carry
        lax.fori_loop(0, blk, one, 0, unroll=16)

    def wait_rows(slot):
        pltpu.make_async_copy(h_hbm.at[pl.ds(0, blk * nc), :], xbuf.at[slot], rsem.at[slot]).wait()

    @pl.when(jnp.logical_and(i == 0, nact > 0))
    def _():
        idx_copy(0, 0).start()
        idx_copy(0, 0).wait()
        issue_rows(0)

        @pl.when(nact > 1)
        def _():
            idx_copy(1, 1).start()

    cur = lax.rem(i, 2)
    nxt = 1 - cur

    has_next = i + 1 < nact

    @pl.when(has_next)
    def _():
        idx_copy(i + 1, nxt).wait()

    @pl.when(i < nact)
    def _():
        wait_rows(cur)
        for j in range(blk):
            @pl.when(has_next)
            def _():
                src = pl.multiple_of(idx_sm[nxt, j], nc)
                pltpu.make_async_copy(h_hbm.at[pl.ds(src, nc), :], xbuf.at[nxt, pl.ds(j * nc, nc), :],
                                      rsem.at[nxt]).start()
        x = _load_token_tiles(xbuf.at[cur], nc).astype(BF16)
        ug = jnp.dot(x, w1g_ref[0], preferred_element_type=F32) + b1g_ref[0]
        ul = jnp.dot(x, w1l_ref[0], preferred_element_type=F32) + b1l_ref[0]
        x_glu = jnp.minimum(ug, SWIGLU_LIMIT)
        x_lin = jnp.clip(ul, -SWIGLU_LIMIT, SWIGLU_LIMIT)
        act = (x_lin + 1.0) * (x_glu * (1.0 / (1.0 + jnp.exp(-SWIGLU_ALPHA * x_glu))))
        y = jnp.dot(act.astype(BF16), w2_ref[0], preferred_element_type=F32) + b2_ref[0]
        _store_token_tiles(y_ref, y)

    @pl.when(i + 2 < nact)
    def _():
        idx_copy(i + 2, cur).start()

    @pl.when(i >= nact)
    def _():
        y_ref[...] = jnp.zeros(y_ref.shape, F32)


def _experts(h2, slot_tok, block_e, nact, p, nc):
    nb, blk = slot_tok.shape
    wspec = lambda a: pl.BlockSpec((1,) + a.shape[1:], lambda i, be, na: (be[i], 0, 0))
    grid_spec = pltpu.PrefetchScalarGridSpec(
        num_scalar_prefetch=2,
        grid=(nb,),
        in_specs=[pl.BlockSpec(memory_space=pl.ANY), pl.BlockSpec(memory_space=pl.ANY),
                  wspec(p['w1g']), wspec(p['w1l']), wspec(p['w2']),
                  wspec(p['b1g']), wspec(p['b1l']), wspec(p['b2'])],
        out_specs=pl.BlockSpec((blk * nc, LANE), lambda i, be, na: (i, 0)),
        scratch_shapes=[pltpu.SMEM((2, blk), jnp.int32), pltpu.VMEM((2, blk * nc, LANE), F32),
                        pltpu.SemaphoreType.DMA((2,)), pltpu.SemaphoreType.DMA((2,))],
    )
    return pl.pallas_call(
        functools.partial(_experts_kernel, blk=blk, nc=nc),
        grid_spec=grid_spec,
        out_shape=jax.ShapeDtypeStruct((nb * blk * nc, LANE), F32),
        compiler_params=_cparams(("arbitrary",)),
        name="experts",
    )(block_e, nact, slot_tok, h2, p['w1g'], p['w1l'], p['w2'], p['b1g'], p['b1l'], p['b2'])


def _deint_kernel(w_ref, pg_ref, pl_ref, g_ref, l_ref):
    w = w_ref[0].astype(BF16)
    g_ref[0] = jnp.dot(w, pg_ref[...], preferred_element_type=F32).astype(BF16)
    l_ref[0] = jnp.dot(w, pl_ref[...], preferred_element_type=F32).astype(BF16)


def _deinterleave_w1(w_mlp1):
    ne, d, f2 = w_mlp1.shape
    f = f2 // 2
    rows = min(512, d)
    src = lax.broadcasted_iota(jnp.int32, (f2, f), 0)
    dst = lax.broadcasted_iota(jnp.int32, (f2, f), 1)
    pg = (src == 2 * dst).astype(BF16)
    pl_ = (src == 2 * dst + 1).astype(BF16)
    return pl.pallas_call(
        _deint_kernel,
        grid=(ne, d // rows),
        in_specs=[pl.BlockSpec((1, rows, f2), lambda e, r: (e, r, 0)),
                  pl.BlockSpec((f2, f), lambda e, r: (0, 0)),
                  pl.BlockSpec((f2, f), lambda e, r: (0, 0))],
        out_specs=[pl.BlockSpec((1, rows, f), lambda e, r: (e, r, 0))] * 2,
        out_shape=[jax.ShapeDtypeStruct((ne, d, f), BF16)] * 2,
        compiler_params=_cparams(("arbitrary", "arbitrary")),
        name="deint",
    )(w_mlp1, pg, pl_)


def _combine_kernel(dest_hbm, y_hbm, x1_ref, g_ref, ada_ref, ln_ref, o_ref,
                    idx_sm, ybuf, isem, rsem, *, t, nsteps, nc):
    i = pl.program_id(0) * pl.num_programs(1) + pl.program_id(1)

    def idx_copy(step, slot):
        return pltpu.make_async_copy(dest_hbm.at[step], idx_sm.at[slot], isem.at[slot])

    def issue_rows(slot):
        def one(j, carry):
            for kk in range(TOP_K):
                src = pl.multiple_of(idx_sm[slot, j * TOP_K + kk], nc)
                dst = pl.multiple_of(j * nc, nc)
                pltpu.make_async_copy(y_hbm.at[pl.ds(src, nc), :], ybuf.at[slot, kk, pl.ds(dst, nc), :],
                                      rsem.at[slot]).start()
            return carry
        lax.fori_loop(0, t, one, 0, unroll=4)

    def wait_rows(slot):
        for kk in range(TOP_K):
            pltpu.make_async_copy(y_hbm.at[pl.ds(0, t * nc), :], ybuf.at[slot, kk], rsem.at[slot]).wait()

    @pl.when(i == 0)
    def _():
        idx_copy(0, 0).start()
        idx_copy(0, 0).wait()
        issue_rows(0)
        if nsteps > 1:
            idx_copy(1, 1).start()

    cur = lax.rem(i, 2)
    nxt = 1 - cur

    has_next = i + 1 < nsteps

    @pl.when(has_next)
    def _():
        idx_copy(i + 1, nxt).wait()

    for j in range(t):
        for kk in range(TOP_K):
            @pl.when(has_next)
            def _():
                src = pl.multiple_of(idx_sm[nxt, j * TOP_K + kk], nc)
                pltpu.make_async_copy(y_hbm.at[pl.ds(src, nc), :], ybuf.at[nxt, kk, pl.ds(j * nc, nc), :],
                                      rsem.at[nxt]).start()

    @pl.when(i + 2 < nsteps)
    def _():
        idx_copy(i + 2, cur).start()

    wait_rows(cur)
    g = g_ref[0]
    y = jnp.zeros((t, x1_ref.shape[2]), F32)
    for kk in range(TOP_K):
        y = y + _load_token_tiles(ybuf.at[cur, kk], nc) * g[:, kk:kk + 1]
    g_m = ada_ref[0, 5:6, :]
    o_ref[0] = x1_ref[0] + g_m * (_rms(y) * ln_ref[...])


def _combine(dest, y_sorted, x1, gates, ada3, ln_post_moe):
    b, s, d = x1.shape
    t = min(T_COMB, s)
    ns = s // t
    nsteps = b * ns
    nc = d // LANE
    dest2 = (dest * nc).reshape(nsteps, t * TOP_K)
    return pl.pallas_call(
        functools.partial(_combine_kernel, t=t, nsteps=nsteps, nc=nc),
        grid=(b, ns),
        in_specs=[pl.BlockSpec(memory_space=pl.ANY), pl.BlockSpec(memory_space=pl.ANY),
                  pl.BlockSpec((1, t, d), lambda i, j: (i, j, 0)),
                  pl.BlockSpec((1, t, LANE), lambda i, j: (i, j, 0)),
                  pl.BlockSpec((1, 6, d), lambda i, j: (i, 0, 0)),
                  pl.BlockSpec(ln_post_moe.shape, lambda i, j: (0, 0))],
        out_specs=pl.BlockSpec((1, t, d), lambda i, j: (i, j, 0)),
        out_shape=jax.ShapeDtypeStruct((b, s, d), F32),
        scratch_shapes=[pltpu.SMEM((2, t * TOP_K), jnp.int32),
                        pltpu.VMEM((2, TOP_K, t * nc, LANE), F32),
                        pltpu.SemaphoreType.DMA((2,)), pltpu.SemaphoreType.DMA((2,))],
        compiler_params=_cparams(("arbitrary", "arbitrary")),
        name="combine",
    )(dest2, y_sorted, x1, gates, ada3, ln_post_moe)


def _rope_partner(w):
    half = MLA_ROPE // 2
    return jnp.concatenate([-w[..., half:], w[..., :half]], axis=-1)


def _prep_params(ln_pre_mix, ln_post_mix, ln_pre_moe, ln_post_moe, w_in, na_rpb, q_a_norm, w_q_b,
                 kv_a_norm, w_kv_b, g_na_out, g_mla_out, w_o, w_router, b_router,
                 w_mlp1, b_mlp1, w_mlp2, b_mlp2):
    d = w_in.shape[0]
    na_w = NA_HEADS * HEAD_DIM
    q_lora = w_q_b.shape[0]
    kv_lora = w_kv_b.shape[0]
    qk = MLA_NOPE + MLA_ROPE
    main_w = 3 * na_w + q_lora + kv_lora
    w_kr = w_in[:, main_w:main_w + MLA_ROPE]
    zpad = lambda n: jnp.zeros((d, n), F32)
    kra = jnp.concatenate([zpad(MLA_NOPE), w_kr, zpad(HEAD_PAD - qk)], axis=1)
    krb = jnp.concatenate([zpad(MLA_NOPE), _rope_partner(w_kr), zpad(HEAD_PAD - qk)], axis=1)
    w1 = jnp.concatenate([w_in[:, :main_w], kra, krb], axis=1).astype(BF16)

    wq = w_q_b.reshape(q_lora, MLA_HEADS, qk)
    zq = lambda n: jnp.zeros((q_lora, MLA_HEADS, n), F32)
    wqa = jnp.concatenate([wq, zq(HEAD_PAD - qk)], axis=2).reshape(q_lora, -1).astype(BF16)
    wqb = jnp.concatenate([zq(MLA_NOPE), _rope_partner(wq[:, :, MLA_NOPE:]), zq(HEAD_PAD - qk)],
                          axis=2).reshape(q_lora, -1).astype(BF16)
    wkv = w_kv_b.reshape(kv_lora, MLA_HEADS, MLA_NOPE + MLA_V)
    zk = lambda n: jnp.zeros((kv_lora, MLA_HEADS, n), F32)
    wk = jnp.concatenate([wkv[:, :, :MLA_NOPE], zk(HEAD_PAD - MLA_NOPE)], axis=2).reshape(kv_lora, -1).astype(BF16)
    wv = jnp.concatenate([wkv[:, :, MLA_NOPE:], zk(HEAD_PAD - MLA_V)], axis=2).reshape(kv_lora, -1).astype(BF16)

    mla_w = MLA_HEADS * MLA_V
    g_mla_pad = jnp.concatenate([g_mla_out.reshape(MLA_HEADS, MLA_V),
                                 jnp.zeros((MLA_HEADS, HEAD_PAD - MLA_V), F32)], axis=1).reshape(1, -1)
    wo_mla = w_o[na_w:na_w + mla_w].reshape(MLA_HEADS, MLA_V, d)
    wo_mla = jnp.concatenate([wo_mla, jnp.zeros((MLA_HEADS, HEAD_PAD - MLA_V, d), F32)],
                             axis=1).reshape(MLA_HEADS * HEAD_PAD, d).astype(BF16)
    w_r = jnp.concatenate([w_router, jnp.zeros((d, LANE - N_EXPERTS), F32)], axis=1)
    w_r_hi = w_r.astype(BF16)
    w_r_lo = (w_r - w_r_hi.astype(F32)).astype(BF16)
    w_r = jnp.concatenate([w_r_hi, w_r_hi, w_r_lo], axis=0)
    b_r = jnp.concatenate([b_router, jnp.zeros((LANE - N_EXPERTS,), F32)]).reshape(1, LANE)
    ne = w_mlp1.shape[0]
    w1g, w1l = _deinterleave_w1(w_mlp1)
    return {
        'ln_pre_mix': ln_pre_mix.reshape(1, d), 'ln_post_mix': ln_post_mix.reshape(1, d),
        'ln_pre_moe': ln_pre_moe.reshape(1, d), 'ln_post_moe': ln_post_moe.reshape(1, d),
        'w1': w1, 'q_a_norm': q_a_norm.reshape(1, q_lora), 'wqa': wqa, 'wqb': wqb,
        'kv_a_norm': kv_a_norm.reshape(1, kv_lora), 'wk': wk, 'wv': wv,
        'na_bias': _na_bias_table(na_rpb),
        'g_na_out': g_na_out.reshape(1, na_w), 'g_mla_pad': g_mla_pad,
        'wo_na': w_o[:na_w].astype(BF16), 'wo_mla': wo_mla,
        'w_router': w_r, 'b_router': b_r,
        'w1g': w1g, 'w1l': w1l,
        'w2': w_mlp2.astype(BF16),
        'b1g': b_mlp1[:, 0::2].reshape(ne, 1, -1), 'b1l': b_mlp1[:, 1::2].reshape(ne, 1, -1),
        'b2': b_mlp2.reshape(ne, 1, -1),
    }


def _rope_lane_tables(s):
    half = MLA_ROPE // 2
    inv = ROPE_THETA ** (-jnp.arange(half, dtype=F32) / half)
    ang = jnp.arange(s, dtype=F32)[:, None] * inv[None, :]
    cos, sin = jnp.cos(ang), jnp.sin(ang)
    z = lambda n: jnp.zeros((s, n), F32)
    tail = HEAD_PAD - MLA_NOPE - MLA_ROPE
    ck = jnp.concatenate([z(MLA_NOPE), cos, cos, z(tail)], axis=1)
    sk = jnp.concatenate([z(MLA_NOPE), sin, sin, z(tail)], axis=1)
    return ck, sk


def _moe_plan(e4, rank4, counts, n):
    nk = n * TOP_K
    padded = (counts + MOE_BLK - 1) // MOE_BLK * MOE_BLK
    pad_end = jnp.cumsum(padded)
    pad_start = pad_end - padded
    dest = pad_start[e4] + rank4
    nb = (nk + N_EXPERTS * (MOE_BLK - 1) + MOE_BLK - 1) // MOE_BLK
    tok = jnp.broadcast_to(jnp.arange(n, dtype=jnp.int32)[:, None], (n, TOP_K))
    slot_tok = jnp.zeros((nb * MOE_BLK,), jnp.int32).at[dest.reshape(-1)].set(
        tok.reshape(-1), unique_indices=True, mode='promise_in_bounds')
    block_start = jnp.arange(nb, dtype=jnp.int32) * MOE_BLK
    block_e = jnp.minimum(jnp.sum((pad_end[None, :] <= block_start[:, None]).astype(jnp.int32), axis=1),
                          N_EXPERTS - 1)
    nact = (pad_end[-1] // MOE_BLK).astype(jnp.int32).reshape(1)
    return dest.astype(jnp.int32), slot_tok.reshape(nb, MOE_BLK), block_e, nact


def _encoder_layer(x, ada3, p):
    b, s, d = x.shape
    ck, sk = _rope_lane_tables(s)
    qna, kna, vna, q, k, v = _inproj(x, ada3, p, ck, sk)
    na_out = _na(qna, kna, vna, p['na_bias'])
    mla_out = _flash(q, k, v)
    x1, h2, e, g, rank, cnt = _post(na_out, mla_out, x, ada3, p)
    n = b * s
    e4 = e.reshape(n, LANE)[:, :TOP_K]
    rank4 = rank.reshape(n, LANE)[:, :TOP_K]
    counts = cnt[0, :N_EXPERTS].astype(jnp.int32)
    dest, slot_tok, block_e, nact = _moe_plan(e4, rank4, counts, n)
    nc = d // LANE
    y_sorted = _experts(h2.reshape(n * nc, LANE), slot_tok * nc, block_e, nact, p, nc)
    return _combine(dest, y_sorted, x1, g, ada3, p['ln_post_moe'])


def kernel(x_prompt, x_sample, c_prompt, c_sample, ln_pre_mix, ln_post_mix, ln_pre_moe, ln_post_moe, w_ada, b_ada, w_in, na_rpb, q_a_norm, w_q_b, kv_a_norm, w_kv_b, g_na_out, g_mla_out, w_o, w_router, b_router, w_mlp1, b_mlp1, w_mlp2, b_mlp2):
    depth = w_ada.shape[0]
    d = x_prompt.shape[-1]
    bp, bs = x_prompt.shape[0], x_sample.shape[0]
    rows = -(-(bp + bs) // 8) * 8
    y_prompt, y_sample = x_prompt, x_sample
    for l in range(depth):
        p = _prep_params(ln_pre_mix[l], ln_post_mix[l], ln_pre_moe[l], ln_post_moe[l], w_in[l], na_rpb[l],
                         q_a_norm[l], w_q_b[l], kv_a_norm[l], w_kv_b[l], g_na_out[l], g_mla_out[l], w_o[l],
                         w_router[l], b_router[l], w_mlp1[l], b_mlp1[l], w_mlp2[l], b_mlp2[l])
        c_all = jnp.concatenate([c_prompt, c_sample, jnp.zeros((rows - bp - bs, d), F32)], axis=0)
        ada3 = _ada(c_all, w_ada[l], b_ada[l]).reshape(rows, 6, d)
        y_prompt = _encoder_layer(y_prompt, ada3[:bp], p)
        y_sample = _encoder_layer(y_sample, ada3[bp:bp + bs], p)
    return (y_prompt, y_sample)
```

```python
import functools

import numpy as np
import jax
import jax.numpy as jnp
from jax import lax
from jax.experimental import pallas as pl
from jax.experimental.pallas import tpu as pltpu

F32 = jnp.float32
BF16 = jnp.bfloat16

GRID_W = 64
HEAD_DIM = 64
NA_HEADS = 8
NA_KR = 8
NA_KC = 16
MLA_HEADS = 8
MLA_NOPE = 64
MLA_ROPE = 32
MLA_V = 64
ROPE_THETA = 10000.0
N_EXPERTS = 32
TOP_K = 4
SWIGLU_LIMIT = 7.0
SWIGLU_ALPHA = 1.702
EPS = 1e-6

LANE = 128
HEAD_PAD = 128
NEG_BIG = -1e30
LOG2E = 1.4426950408889634

T_PROJ = 256
T_Q = 512
T_K = 512
NA_ROWS = 8
T_POST = 256
MOE_BLK = 512
T_COMB = 128
VMEM_LIMIT = 56 * 1024 * 1024


def _rms(x, n=None):
    n = x.shape[-1] if n is None else n
    return x * lax.rsqrt(jnp.sum(x * x, axis=-1, keepdims=True) * (1.0 / n) + EPS)


def _cparams(sem):
    return pltpu.CompilerParams(dimension_semantics=sem, vmem_limit_bytes=VMEM_LIMIT)


def _store_token_tiles(ref, x):
    rows, d = x.shape
    nc = d // LANE
    for c in range(nc):
        ref[pl.ds(c, rows, stride=nc), :] = x[:, c * LANE:(c + 1) * LANE]


def _load_token_tiles(ref, nc):
    rows = ref.shape[0] // nc
    return jnp.concatenate([ref[pl.ds(c, rows, stride=nc), :] for c in range(nc)], axis=1)


def _ada_kernel(c_ref, w_ref, b_ref, o_ref):
    c = c_ref[...]
    s = c / (1.0 + jnp.exp(-c))
    o_ref[...] = jnp.dot(s, w_ref[...], preferred_element_type=F32,
                         precision=lax.Precision.HIGHEST) + b_ref[...]


def _ada(c_all, w_ada, b_ada):
    rows, d = c_all.shape
    n_out = w_ada.shape[1]
    return pl.pallas_call(
        _ada_kernel,
        grid=(n_out // d,),
        in_specs=[pl.BlockSpec((rows, d), lambda j: (0, 0)),
                  pl.BlockSpec((d, d), lambda j: (0, j)),
                  pl.BlockSpec((1, d), lambda j: (0, j))],
        out_specs=pl.BlockSpec((rows, d), lambda j: (0, j)),
        out_shape=jax.ShapeDtypeStruct((rows, n_out), F32),
        compiler_params=_cparams(("arbitrary",)),
        name="ada",
    )(c_all, w_ada, b_ada.reshape(1, n_out))


def _inproj_kernel(x_ref, ada_ref, g_ref, ck_ref, sk_ref, w1_ref, qan_ref, wqa_ref, wqb_ref,
                   kvan_ref, wk_ref, wv_ref,
                   qna_ref, kna_ref, vna_ref, q_ref, k_ref, v_ref, *, na_w, q_lora, kv_lora, qscale):
    x = x_ref[0]
    sh = ada_ref[0, 0:1, :]
    sc = ada_ref[0, 1:2, :]
    h = (_rms(x) * g_ref[...]) * (1.0 + sc) + sh
    proj = jnp.dot(h.astype(BF16), w1_ref[...], preferred_element_type=F32)
    qna_ref[0] = (proj[:, 0:na_w] * (1.0 / float(np.sqrt(HEAD_DIM)))).astype(BF16)
    kna_ref[0] = proj[:, na_w:2 * na_w].astype(BF16)
    vna_ref[0] = proj[:, 2 * na_w:3 * na_w].astype(BF16)
    o = 3 * na_w
    cqn = (_rms(proj[:, o:o + q_lora]) * qan_ref[...]).astype(BF16)
    o += q_lora
    ckvn = (_rms(proj[:, o:o + kv_lora]) * kvan_ref[...]).astype(BF16)
    o += kv_lora
    kra = proj[:, o:o + HEAD_PAD]
    krb = proj[:, o + HEAD_PAD:o + 2 * HEAD_PAD]
    qa = jnp.dot(cqn, wqa_ref[...], preferred_element_type=F32)
    qb = jnp.dot(cqn, wqb_ref[...], preferred_element_type=F32)
    kn = jnp.dot(ckvn, wk_ref[...], preferred_element_type=F32)
    vv = jnp.dot(ckvn, wv_ref[...], preferred_element_type=F32)
    ck = ck_ref[...]
    sk = sk_ref[...]
    lane = lax.broadcasted_iota(jnp.int32, (1, HEAD_PAD), 1)
    nope = (lane < MLA_NOPE).astype(F32)
    ones_col = (lane == MLA_V).astype(F32)
    cq = (nope + ck) * qscale
    sq = sk * qscale
    kpe = kra * ck + krb * sk
    for hd in range(MLA_HEADS):
        sl = slice(hd * HEAD_PAD, (hd + 1) * HEAD_PAD)
        q_ref[0, :, sl] = (qa[:, sl] * cq + qb[:, sl] * sq).astype(BF16)
        k_ref[0, :, sl] = (kn[:, sl] + kpe).astype(BF16)
        v_ref[0, :, sl] = (vv[:, sl] + ones_col).astype(BF16)


def _inproj(x, ada3, p, ck, sk):
    b, s, d = x.shape
    t = min(T_PROJ, s)
    na_w = NA_HEADS * HEAD_DIM
    hw = MLA_HEADS * HEAD_PAD
    q_lora = p['wqa'].shape[0]
    kv_lora = p['wk'].shape[0]
    w1 = p['w1']
    kern = functools.partial(_inproj_kernel, na_w=na_w, q_lora=q_lora, kv_lora=kv_lora,
                             qscale=LOG2E / float(np.sqrt(MLA_NOPE + MLA_ROPE)))
    full = lambda a: pl.BlockSpec(a.shape, lambda i, j: (0,) * a.ndim)
    tok = lambda w: pl.BlockSpec((1, t, w), lambda i, j: (i, j, 0))
    outs = pl.pallas_call(
        kern,
        grid=(b, s // t),
        in_specs=[tok(d),
                  pl.BlockSpec((1, 6, d), lambda i, j: (i, 0, 0)),
                  full(p['ln_pre_mix']),
                  pl.BlockSpec((t, HEAD_PAD), lambda i, j: (j, 0)),
                  pl.BlockSpec((t, HEAD_PAD), lambda i, j: (j, 0)),
                  full(w1), full(p['q_a_norm']), full(p['wqa']), full(p['wqb']),
                  full(p['kv_a_norm']), full(p['wk']), full(p['wv'])],
        out_specs=[tok(na_w), tok(na_w), tok(na_w), tok(hw), tok(hw), tok(hw)],
        out_shape=[jax.ShapeDtypeStruct((b, s, na_w), BF16)] * 3
                  + [jax.ShapeDtypeStruct((b, s, hw), BF16)] * 3,
        compiler_params=_cparams(("arbitrary", "arbitrary")),
        name="inproj",
    )(x, ada3, p['ln_pre_mix'], ck, sk, w1, p['q_a_norm'], p['wqa'], p['wqb'],
      p['kv_a_norm'], p['wk'], p['wv'])
    return outs


def _na_kernel(q_ref, kp_ref, kc_ref, kn_ref, vp_ref, vc_ref, vn_ref, bias_ref, o_ref,
               kcat, vcat, *, rows):
    blk = pl.program_id(1)
    nb = NA_ROWS * GRID_W
    win = NA_KR * GRID_W
    for i, (kr, vr) in enumerate(((kp_ref, vp_ref), (kc_ref, vc_ref), (kn_ref, vn_ref))):
        kcat[i * nb:(i + 1) * nb, :] = kr[0]
        vcat[i * nb:(i + 1) * nb, :] = vr[0]
    lane = lax.broadcasted_iota(jnp.int32, (GRID_W, LANE), 1)
    lo = lane < HEAD_DIM

    def row_body(rl, carry):
        r = blk * NA_ROWS + rl
        rs = jnp.clip(r - NA_KR // 2, 0, rows - NA_KR)
        d = r - rs
        off = pl.multiple_of((rs - (blk - 1) * NA_ROWS) * GRID_W, GRID_W)
        qoff = pl.multiple_of(rl * GRID_W, GRID_W)
        scores = []
        for hp in range(NA_HEADS // 2):
            sl = slice(hp * LANE, (hp + 1) * LANE)
            qp = q_ref[0, pl.ds(qoff, GRID_W), sl]
            kw = kcat[pl.ds(off, win), sl]
            zero = jnp.zeros_like(qp)
            qm = jnp.concatenate([jnp.where(lo, qp, zero), jnp.where(lo, zero, qp)], axis=0)
            s = lax.dot_general(qm, kw, (((1,), (1,)), ((), ())), preferred_element_type=F32)
            scores.append(s + bias_ref[d, hp])
        probs, inv_l = [], []
        for s in scores:
            m = jnp.max(s, axis=-1, keepdims=True)
            e = jnp.exp(s - m)
            inv_l.append(1.0 / jnp.sum(e, axis=-1, keepdims=True))
            probs.append(e.astype(BF16))
        pair_outs = []
        for hp in range(NA_HEADS // 2):
            vw = vcat[pl.ds(off, win), hp * LANE:(hp + 1) * LANE]
            o2 = jnp.dot(probs[hp], vw, preferred_element_type=F32) * inv_l[hp]
            pair_outs.append(jnp.where(lo, o2[:GRID_W], o2[GRID_W:]))
        o_ref[0, pl.ds(qoff, GRID_W), :] = jnp.concatenate(pair_outs, axis=1)
        return carry

    lax.fori_loop(0, NA_ROWS, row_body, 0)


def _na_bias_table(rpb):
    nh, nrow, nrel = rpb.shape
    cols = np.arange(GRID_W)
    cstart = np.clip(cols - NA_KC // 2, 0, GRID_W - NA_KC)
    j = np.arange(GRID_W)
    inwin = (j[None, :] >= cstart[:, None]) & (j[None, :] < cstart[:, None] + NA_KC)
    period = 2 * GRID_W - 1
    vpad = jnp.concatenate([rpb[..., NA_KC - 1:], jnp.zeros((nh, nrow, period - nrel), rpb.dtype),
                            rpb[..., :NA_KC - 1]], axis=-1)
    toep = jnp.tile(vpad, (1, 1, GRID_W))[..., :GRID_W * (period - 1)]
    toep = toep.reshape(nh, nrow, GRID_W, period - 1)[..., :GRID_W]
    tabs = []
    for d in range(NA_KR):
        t = toep[:, NA_KR - 1 - d:2 * NA_KR - 1 - d]
        tabs.append(t.transpose(0, 2, 1, 3))
    tab = jnp.stack(tabs, axis=0).astype(F32)
    tab = jnp.where(jnp.asarray(inwin)[None, None, :, None, :], tab, NEG_BIG)
    return tab.reshape(NA_KR, nh // 2, 2 * GRID_W, NA_KR * GRID_W)


def _na(q, k, v, bias):
    b, s, w = q.shape
    rows = s // GRID_W
    assert rows % NA_ROWS == 0 and rows >= 2 * NA_ROWS
    nblk = rows // NA_ROWS
    nb = NA_ROWS * GRID_W
    cur = pl.BlockSpec((1, nb, w), lambda i, j: (i, j, 0))
    prv = pl.BlockSpec((1, nb, w), lambda i, j: (i, jnp.maximum(j - 1, 0), 0))
    nxt = pl.BlockSpec((1, nb, w), lambda i, j: (i, jnp.minimum(j + 1, nblk - 1), 0))
    return pl.pallas_call(
        functools.partial(_na_kernel, rows=rows),
        grid=(b, nblk),
        in_specs=[cur, prv, cur, nxt, prv, cur, nxt,
                  pl.BlockSpec(bias.shape, lambda i, j: (0, 0, 0, 0))],
        out_specs=cur,
        out_shape=jax.ShapeDtypeStruct((b, s, w), F32),
        scratch_shapes=[pltpu.VMEM((3 * nb, w), BF16), pltpu.VMEM((3 * nb, w), BF16)],
        compiler_params=_cparams(("arbitrary", "arbitrary")),
        name="na",
    )(q, k, k, k, v, v, v, bias)


def _flash_kernel(q_ref, k_ref, v_ref, o_ref, m_sc, acc_sc, s_sc, *, tk, nk, unroll):
    q = q_ref[0]
    m_sc[...] = jnp.full(m_sc.shape, -jnp.inf, F32)
    acc_sc[...] = jnp.zeros(acc_sc.shape, F32)

    def scores(j):
        off = pl.multiple_of(j * tk, tk)
        ks = k_ref[0, pl.ds(off, tk), :]
        return lax.dot_general(q, ks, (((1,), (1,)), ((), ())), preferred_element_type=F32)

    def accumulate(s, j):
        off = pl.multiple_of(j * tk, tk)
        vs = v_ref[0, pl.ds(off, tk), :]
        cols = [s[:, c * LANE:(c + 1) * LANE] for c in range(tk // LANE)]
        cm = cols[0]
        for sc in cols[1:]:
            cm = jnp.maximum(cm, sc)
        m_prev = m_sc[...]
        m_new = jnp.maximum(m_prev, jnp.max(cm, axis=-1, keepdims=True))
        alpha = jnp.exp2(m_prev - m_new)
        p = jnp.concatenate([jnp.exp2((sc - m_new).astype(BF16)) for sc in cols], axis=1)
        acc_sc[...] = alpha * acc_sc[...] + jnp.dot(p, vs, preferred_element_type=F32)
        m_sc[...] = m_new

    s_sc[...] = scores(0)

    def trip(i, carry):
        s_cur = s_sc[...]
        for u in range(unroll):
            j = i * unroll + u
            s_next = scores(jnp.minimum(j + 1, nk - 1))
            accumulate(s_cur, j)
            s_cur = s_next
        s_sc[...] = s_cur
        return carry

    lax.fori_loop(0, nk // unroll, trip, 0)
    acc = acc_sc[...]
    o_ref[0] = acc / acc[:, MLA_V:MLA_V + 1]


def _flash(q, k, v):
    b, s, hw = q.shape
    nh = hw // HEAD_PAD
    tq = min(T_Q, s)
    tk = min(T_K, s)
    nk = s // tk
    unroll = 4 if nk % 4 == 0 else (2 if nk % 2 == 0 else 1)
    return pl.pallas_call(
        functools.partial(_flash_kernel, tk=tk, nk=nk, unroll=unroll),
        grid=(b, nh, s // tq),
        in_specs=[pl.BlockSpec((1, tq, HEAD_PAD), lambda i, h, j: (i, j, h)),
                  pl.BlockSpec((1, s, HEAD_PAD), lambda i, h, j: (i, 0, h)),
                  pl.BlockSpec((1, s, HEAD_PAD), lambda i, h, j: (i, 0, h))],
        out_specs=pl.BlockSpec((1, tq, HEAD_PAD), lambda i, h, j: (i, j, h)),
        out_shape=jax.ShapeDtypeStruct((b, s, hw), F32),
        scratch_shapes=[pltpu.VMEM((tq, HEAD_PAD), F32), pltpu.VMEM((tq, HEAD_PAD), F32),
                        pltpu.VMEM((tq, tk), F32)],
        compiler_params=_cparams(("arbitrary", "arbitrary", "arbitrary")),
        name="flash",
    )(q, k, v)


def _post_kernel(na_ref, mla_ref, x_ref, ada_ref, gna_ref, gmla_ref, wona_ref, womla_ref,
                 lnpost_ref, lnpre_ref, wr_ref, br_ref,
                 x1_ref, h2_ref, e_ref, g_ref, rank_ref, cnt_ref, base_sc):
    first = jnp.logical_and(pl.program_id(0) == 0, pl.program_id(1) == 0)

    @pl.when(first)
    def _():
        base_sc[...] = jnp.zeros(base_sc.shape, F32)

    t = x_ref.shape[1]
    na = na_ref[0]
    nan_ = _rms(na) * gna_ref[...]
    ml = mla_ref[0]
    lane_w = lax.broadcasted_iota(jnp.int32, ml.shape, 1)
    ml = jnp.where((lane_w % HEAD_PAD) < MLA_V, ml, 0.0)
    mln = _rms(ml, n=MLA_HEADS * MLA_V) * gmla_ref[...]
    mix = (jnp.dot(nan_.astype(BF16), wona_ref[...], preferred_element_type=F32)
           + jnp.dot(mln.astype(BF16), womla_ref[...], preferred_element_type=F32))
    g_a = ada_ref[0, 2:3, :]
    sh_m = ada_ref[0, 3:4, :]
    sc_m = ada_ref[0, 4:5, :]
    x1 = x_ref[0] + g_a * (_rms(mix) * lnpost_ref[...])
    x1_ref[0] = x1
    h2 = (_rms(x1) * lnpre_ref[...]) * (1.0 + sc_m) + sh_m
    _store_token_tiles(h2_ref.at[0], h2)

    h_hi = h2.astype(BF16)
    h_lo = (h2 - h_hi.astype(F32)).astype(BF16)
    logits = jnp.dot(jnp.concatenate([h_hi, h_lo, h_hi], axis=1), wr_ref[...],
                     preferred_element_type=F32) + br_ref[...]
    lane = lax.broadcasted_iota(jnp.int32, (t, LANE), 1)
    lane_f = lane.astype(F32)
    work = jnp.where(lane < N_EXPERTS, logits, -jnp.inf)
    vals, idxs = [], []
    onehot = jnp.zeros((t, LANE), F32)
    for _ in range(TOP_K):
        mk = jnp.max(work, axis=-1, keepdims=True)
        ik = jnp.min(jnp.where(work == mk, lane_f, float(LANE)), axis=-1, keepdims=True)
        sel = lane_f == ik
        work = jnp.where(sel, -jnp.inf, work)
        onehot = jnp.where(sel, 1.0, onehot)
        vals.append(mk)
        idxs.append(ik)
    exps = [jnp.exp(vk - vals[0]) for vk in vals]
    denom = exps[0]
    for ek in exps[1:]:
        denom = denom + ek

    r_i = lax.broadcasted_iota(jnp.int32, (t, t), 0)
    c_i = lax.broadcasted_iota(jnp.int32, (t, t), 1)
    tril = jnp.where(c_i < r_i, 1.0, 0.0).astype(BF16)
    pref = jnp.dot(tril, onehot.astype(BF16), preferred_element_type=F32) + base_sc[...]

    e_out = jnp.zeros((t, LANE), F32)
    g_out = jnp.zeros((t, LANE), F32)
    r_out = jnp.zeros((t, LANE), F32)
    for kk in range(TOP_K):
        rk = jnp.sum(jnp.where(lane_f == idxs[kk], pref, 0.0), axis=-1, keepdims=True)
        here = lane == kk
        e_out = jnp.where(here, idxs[kk], e_out)
        g_out = jnp.where(here, exps[kk] / denom, g_out)
        r_out = jnp.where(here, rk, r_out)
    e_ref[0] = e_out.astype(jnp.int32)
    g_ref[0] = g_out
    rank_ref[0] = r_out.astype(jnp.int32)
    base_sc[...] = base_sc[...] + jnp.sum(onehot, axis=0, keepdims=True)
    cnt_ref[...] = jnp.broadcast_to(base_sc[...], cnt_ref.shape)


def _post(na_out, mla_out, x, ada3, p):
    b, s, d = x.shape
    t = min(T_POST, s)
    full = lambda a: pl.BlockSpec(a.shape, lambda i, j: (0,) * a.ndim)
    tok = lambda w: pl.BlockSpec((1, t, w), lambda i, j: (i, j, 0))
    return pl.pallas_call(
        _post_kernel,
        grid=(b, s // t),
        in_specs=[tok(na_out.shape[-1]), tok(mla_out.shape[-1]), tok(d),
                  pl.BlockSpec((1, 6, d), lambda i, j: (i, 0, 0)),
                  full(p['g_na_out']), full(p['g_mla_pad']), full(p['wo_na']), full(p['wo_mla']),
                  full(p['ln_post_mix']), full(p['ln_pre_moe']), full(p['w_router']), full(p['b_router'])],
        out_specs=[tok(d), pl.BlockSpec((1, t * (d // LANE), LANE), lambda i, j: (i, j, 0)),
                   tok(LANE), tok(LANE), tok(LANE),
                   pl.BlockSpec((8, LANE), lambda i, j: (0, 0))],
        out_shape=[jax.ShapeDtypeStruct((b, s, d), F32), jax.ShapeDtypeStruct((b, s * (d // LANE), LANE), F32),
                   jax.ShapeDtypeStruct((b, s, LANE), jnp.int32), jax.ShapeDtypeStruct((b, s, LANE), F32),
                   jax.ShapeDtypeStruct((b, s, LANE), jnp.int32), jax.ShapeDtypeStruct((8, LANE), F32)],
        scratch_shapes=[pltpu.VMEM((1, LANE), F32)],
        compiler_params=_cparams(("arbitrary", "arbitrary")),
        name="post",
    )(na_out, mla_out, x, ada3, p['g_na_out'], p['g_mla_pad'], p['wo_na'], p['wo_mla'],
      p['ln_post_mix'], p['ln_pre_moe'], p['w_router'], p['b_router'])


def _experts_kernel(be_ref, nact_ref, tok_hbm, h_hbm, w1g_ref, w1l_ref, w2_ref, b1g_ref, b1l_ref, b2_ref,
                    y_ref, idx_sm, xbuf, isem, rsem, *, blk, nc):
    i = pl.program_id(0)
    nact = nact_ref[0]

    def idx_copy(step, slot):
        return pltpu.make_async_copy(tok_hbm.at[step], idx_sm.at[slot], isem.at[slot])

    def issue_rows(slot):
        def one(j, carry):
            src = pl.multiple_of(idx_sm[slot, j], nc)
            dst = pl.multiple_of(j * nc, nc)
            pltpu.make_async_copy(h_hbm.at[pl.ds(src, nc), :], xbuf.at[slot, pl.ds(dst, nc), :],
                                  rsem.at[slot]).start()
            return carry
        lax.fori_loop(0, blk, one, 0, unroll=16)

    def wait_rows(slot):
        pltpu.make_async_copy(h_hbm.at[pl.ds(0, blk * nc), :], xbuf.at[slot], rsem.at[slot]).wait()

    @pl.when(jnp.logical_and(i == 0, nact > 0))
    def _():
        idx_copy(0, 0).start()
        idx_copy(0, 0).wait()
        issue_rows(0)

        @pl.when(nact > 1)
        def _():
            idx_copy(1, 1).start()

    cur = lax.rem(i, 2)
    nxt = 1 - cur

    has_next = i + 1 < nact

    @pl.when(has_next)
    def _():
        idx_copy(i + 1, nxt).wait()

    @pl.when(i < nact)
    def _():
        wait_rows(cur)
        for j in range(blk):
            @pl.when(has_next)
            def _():
                src = pl.multiple_of(idx_sm[nxt, j], nc)
                pltpu.make_async_copy(h_hbm.at[pl.ds(src, nc), :], xbuf.at[nxt, pl.ds(j * nc, nc), :],
                                      rsem.at[nxt]).start()
        x = _load_token_tiles(xbuf.at[cur], nc).astype(BF16)
        ug = jnp.dot(x, w1g_ref[0], preferred_element_type=F32) + b1g_ref[0]
        ul = jnp.dot(x, w1l_ref[0], preferred_element_type=F32) + b1l_ref[0]
        x_glu = jnp.minimum(ug, SWIGLU_LIMIT)
        x_lin = jnp.clip(ul, -SWIGLU_LIMIT, SWIGLU_LIMIT)
        act = (x_lin + 1.0) * (x_glu * (1.0 / (1.0 + jnp.exp(-SWIGLU_ALPHA * x_glu))))
        y = jnp.dot(act.astype(BF16), w2_ref[0], preferred_element_type=F32) + b2_ref[0]
        _store_token_tiles(y_ref, y)

    @pl.when(i + 2 < nact)
    def _():
        idx_copy(i + 2, cur).start()

    @pl.when(i >= nact)
    def _():
        y_ref[...] = jnp.zeros(y_ref.shape, F32)


def _experts(h2, slot_tok, block_e, nact, p, nc):
    nb, blk = slot_tok.shape
    wspec = lambda a: pl.BlockSpec((1,) + a.shape[1:], lambda i, be, na: (be[i], 0, 0))
    grid_spec = pltpu.PrefetchScalarGridSpec(
        num_scalar_prefetch=2,
        grid=(nb,),
        in_specs=[pl.BlockSpec(memory_space=pl.ANY), pl.BlockSpec(memory_space=pl.ANY),
                  wspec(p['w1g']), wspec(p['w1l']), wspec(p['w2']),
                  wspec(p['b1g']), wspec(p['b1l']), wspec(p['b2'])],
        out_specs=pl.BlockSpec((blk * nc, LANE), lambda i, be, na: (i, 0)),
        scratch_shapes=[pltpu.SMEM((2, blk), jnp.int32), pltpu.VMEM((2, blk * nc, LANE), F32),
                        pltpu.SemaphoreType.DMA((2,)), pltpu.SemaphoreType.DMA((2,))],
    )
    return pl.pallas_call(
        functools.partial(_experts_kernel, blk=blk, nc=nc),
        grid_spec=grid_spec,
        out_shape=jax.ShapeDtypeStruct((nb * blk * nc, LANE), F32),
        compiler_params=_cparams(("arbitrary",)),
        name="experts",
    )(block_e, nact, slot_tok, h2, p['w1g'], p['w1l'], p['w2'], p['b1g'], p['b1l'], p['b2'])


def _deint_kernel(w_ref, pg_ref, pl_ref, g_ref, l_ref):
    w = w_ref[0].astype(BF16)
    g_ref[0] = jnp.dot(w, pg_ref[...], preferred_element_type=F32).astype(BF16)
    l_ref[0] = jnp.dot(w, pl_ref[...], preferred_element_type=F32).astype(BF16)


def _deinterleave_w1(w_mlp1):
    ne, d, f2 = w_mlp1.shape
    f = f2 // 2
    rows = min(512, d)
    src = lax.broadcasted_iota(jnp.int32, (f2, f), 0)
    dst = lax.broadcasted_iota(jnp.int32, (f2, f), 1)
    pg = (src == 2 * dst).astype(BF16)
    pl_ = (src == 2 * dst + 1).astype(BF16)
    return pl.pallas_call(
        _deint_kernel,
        grid=(ne, d // rows),
        in_specs=[pl.BlockSpec((1, rows, f2), lambda e, r: (e, r, 0)),
                  pl.BlockSpec((f2, f), lambda e, r: (0, 0)),
                  pl.BlockSpec((f2, f), lambda e, r: (0, 0))],
        out_specs=[pl.BlockSpec((1, rows, f), lambda e, r: (e, r, 0))] * 2,
        out_shape=[jax.ShapeDtypeStruct((ne, d, f), BF16)] * 2,
        compiler_params=_cparams(("arbitrary", "arbitrary")),
        name="deint",
    )(w_mlp1, pg, pl_)


def _combine_kernel(dest_hbm, y_hbm, x1_ref, g_ref, ada_ref, ln_ref, o_ref,
                    idx_sm, ybuf, isem, rsem, *, t, nsteps, nc):
    i = pl.program_id(0) * pl.num_programs(1) + pl.program_id(1)

    def idx_copy(step, slot):
        return pltpu.make_async_copy(dest_hbm.at[step], idx_sm.at[slot], isem.at[slot])

    def issue_rows(slot):
        def one(j, carry):
            for kk in range(TOP_K):
                src = pl.multiple_of(idx_sm[slot, j * TOP_K + kk], nc)
                dst = pl.multiple_of(j * nc, nc)
                pltpu.make_async_copy(y_hbm.at[pl.ds(src, nc), :], ybuf.at[slot, kk, pl.ds(dst, nc), :],
                                      rsem.at[slot]).start()
            return carry
        lax.fori_loop(0, t, one, 0, unroll=4)

    def wait_rows(slot):
        for kk in range(TOP_K):
            pltpu.make_async_copy(y_hbm.at[pl.ds(0, t * nc), :], ybuf.at[slot, kk], rsem.at[slot]).wait()

    @pl.when(i == 0)
    def _():
        idx_copy(0, 0).start()
        idx_copy(0, 0).wait()
        issue_rows(0)
        if nsteps > 1:
            idx_copy(1, 1).start()

    cur = lax.rem(i, 2)
    nxt = 1 - cur

    has_next = i + 1 < nsteps

    @pl.when(has_next)
    def _():
        idx_copy(i + 1, nxt).wait()

    for j in range(t):
        for kk in range(TOP_K):
            @pl.when(has_next)
            def _():
                src = pl.multiple_of(idx_sm[nxt, j * TOP_K + kk], nc)
                pltpu.make_async_copy(y_hbm.at[pl.ds(src, nc), :], ybuf.at[nxt, kk, pl.ds(j * nc, nc), :],
                                      rsem.at[nxt]).start()

    @pl.when(i + 2 < nsteps)
    def _():
        idx_copy(i + 2, cur).start()

    wait_rows(cur)
    g = g_ref[0]
    y = jnp.zeros((t, x1_ref.shape[2]), F32)
    for kk in range(TOP_K):
        y = y + _load_token_tiles(ybuf.at[cur, kk], nc) * g[:, kk:kk + 1]
    g_m = ada_ref[0, 5:6, :]
    o_ref[0] = x1_ref[0] + g_m * (_rms(y) * ln_ref[...])


def _combine(dest, y_sorted, x1, gates, ada3, ln_post_moe):
    b, s, d = x1.shape
    t = min(T_COMB, s)
    ns = s // t
    nsteps = b * ns
    nc = d // LANE
    dest2 = (dest * nc).reshape(nsteps, t * TOP_K)
    return pl.pallas_call(
        functools.partial(_combine_kernel, t=t, nsteps=nsteps, nc=nc),
        grid=(b, ns),
        in_specs=[pl.BlockSpec(memory_space=pl.ANY), pl.BlockSpec(memory_space=pl.ANY),
                  pl.BlockSpec((1, t, d), lambda i, j: (i, j, 0)),
                  pl.BlockSpec((1, t, LANE), lambda i, j: (i, j, 0)),
                  pl.BlockSpec((1, 6, d), lambda i, j: (i, 0, 0)),
                  pl.BlockSpec(ln_post_moe.shape, lambda i, j: (0, 0))],
        out_specs=pl.BlockSpec((1, t, d), lambda i, j: (i, j, 0)),
        out_shape=jax.ShapeDtypeStruct((b, s, d), F32),
        scratch_shapes=[pltpu.SMEM((2, t * TOP_K), jnp.int32),
                        pltpu.VMEM((2, TOP_K, t * nc, LANE), F32),
                        pltpu.SemaphoreType.DMA((2,)), pltpu.SemaphoreType.DMA((2,))],
        compiler_params=_cparams(("arbitrary", "arbitrary")),
        name="combine",
    )(dest2, y_sorted, x1, gates, ada3, ln_post_moe)


def _rope_partner(w):
    half = MLA_ROPE // 2
    return jnp.concatenate([-w[..., half:], w[..., :half]], axis=-1)


def _prep_params(ln_pre_mix, ln_post_mix, ln_pre_moe, ln_post_moe, w_in, na_rpb, q_a_norm, w_q_b,
                 kv_a_norm, w_kv_b, g_na_out, g_mla_out, w_o, w_router, b_router,
                 w_mlp1, b_mlp1, w_mlp2, b_mlp2):
    d = w_in.shape[0]
    na_w = NA_HEADS * HEAD_DIM
    q_lora = w_q_b.shape[0]
    kv_lora = w_kv_b.shape[0]
    qk = MLA_NOPE + MLA_ROPE
    main_w = 3 * na_w + q_lora + kv_lora
    w_kr = w_in[:, main_w:main_w + MLA_ROPE]
    zpad = lambda n: jnp.zeros((d, n), F32)
    kra = jnp.concatenate([zpad(MLA_NOPE), w_kr, zpad(HEAD_PAD - qk)], axis=1)
    krb = jnp.concatenate([zpad(MLA_NOPE), _rope_partner(w_kr), zpad(HEAD_PAD - qk)], axis=1)
    w1 = jnp.concatenate([w_in[:, :main_w], kra, krb], axis=1).astype(BF16)

    wq = w_q_b.reshape(q_lora, MLA_HEADS, qk)
    zq = lambda n: jnp.zeros((q_lora, MLA_HEADS, n), F32)
    wqa = jnp.concatenate([wq, zq(HEAD_PAD - qk)], axis=2).reshape(q_lora, -1).astype(BF16)
    wqb = jnp.concatenate([zq(MLA_NOPE), _rope_partner(wq[:, :, MLA_NOPE:]), zq(HEAD_PAD - qk)],
                          axis=2).reshape(q_lora, -1).astype(BF16)
    wkv = w_kv_b.reshape(kv_lora, MLA_HEADS, MLA_NOPE + MLA_V)
    zk = lambda n: jnp.zeros((kv_lora, MLA_HEADS, n), F32)
    wk = jnp.concatenate([wkv[:, :, :MLA_NOPE], zk(HEAD_PAD - MLA_NOPE)], axis=2).reshape(kv_lora, -1).astype(BF16)
    wv = jnp.concatenate([wkv[:, :, MLA_NOPE:], zk(HEAD_PAD - MLA_V)], axis=2).reshape(kv_lora, -1).astype(BF16)

    mla_w = MLA_HEADS * MLA_V
    g_mla_pad = jnp.concatenate([g_mla_out.reshape(MLA_HEADS, MLA_V),
                                 jnp.zeros((MLA_HEADS, HEAD_PAD - MLA_V), F32)], axis=1).reshape(1, -1)
    wo_mla = w_o[na_w:na_w + mla_w].reshape(MLA_HEADS, MLA_V, d)
    wo_mla = jnp.concatenate([wo_mla, jnp.zeros((MLA_HEADS, HEAD_PAD - MLA_V, d), F32)],
                             axis=1).reshape(MLA_HEADS * HEAD_PAD, d).astype(BF16)
    w_r = jnp.concatenate([w_router, jnp.zeros((d, LANE - N_EXPERTS), F32)], axis=1)
    w_r_hi = w_r.astype(BF16)
    w_r_lo = (w_r - w_r_hi.astype(F32)).astype(BF16)
    w_r = jnp.concatenate([w_r_hi, w_r_hi, w_r_lo], axis=0)
    b_r = jnp.concatenate([b_router, jnp.zeros((LANE - N_EXPERTS,), F32)]).reshape(1, LANE)
    ne = w_mlp1.shape[0]
    w1g, w1l = _deinterleave_w1(w_mlp1)
    return {
        'ln_pre_mix': ln_pre_mix.reshape(1, d), 'ln_post_mix': ln_post_mix.reshape(1, d),
        'ln_pre_moe': ln_pre_moe.reshape(1, d), 'ln_post_moe': ln_post_moe.reshape(1, d),
        'w1': w1, 'q_a_norm': q_a_norm.reshape(1, q_lora), 'wqa': wqa, 'wqb': wqb,
        'kv_a_norm': kv_a_norm.reshape(1, kv_lora), 'wk': wk, 'wv': wv,
        'na_bias': _na_bias_table(na_rpb),
        'g_na_out': g_na_out.reshape(1, na_w), 'g_mla_pad': g_mla_pad,
        'wo_na': w_o[:na_w].astype(BF16), 'wo_mla': wo_mla,
        'w_router': w_r, 'b_router': b_r,
        'w1g': w1g, 'w1l': w1l,
        'w2': w_mlp2.astype(BF16),
        'b1g': b_mlp1[:, 0::2].reshape(ne, 1, -1), 'b1l': b_mlp1[:, 1::2].reshape(ne, 1, -1),
        'b2': b_mlp2.reshape(ne, 1, -1),
    }


def _rope_lane_tables(s):
    half = MLA_ROPE // 2
    inv = ROPE_THETA ** (-jnp.arange(half, dtype=F32) / half)
    ang = jnp.arange(s, dtype=F32)[:, None] * inv[None, :]
    cos, sin = jnp.cos(ang), jnp.sin(ang)
    z = lambda n: jnp.zeros((s, n), F32)
    tail = HEAD_PAD - MLA_NOPE - MLA_ROPE
    ck = jnp.concatenate([z(MLA_NOPE), cos, cos, z(tail)], axis=1)
    sk = jnp.concatenate([z(MLA_NOPE), sin, sin, z(tail)], axis=1)
    return ck, sk


def _moe_plan(e4, rank4, counts, n):
    nk = n * TOP_K
    padded = (counts + MOE_BLK - 1) // MOE_BLK * MOE_BLK
    pad_end = jnp.cumsum(padded)
    pad_start = pad_end - padded
    dest = pad_start[e4] + rank4
    nb = (nk + N_EXPERTS * (MOE_BLK - 1) + MOE_BLK - 1) // MOE_BLK
    tok = jnp.broadcast_to(jnp.arange(n, dtype=jnp.int32)[:, None], (n, TOP_K))
    slot_tok = jnp.zeros((nb * MOE_BLK,), jnp.int32).at[dest.reshape(-1)].set(
        tok.reshape(-1), unique_indices=True, mode='promise_in_bounds')
    block_start = jnp.arange(nb, dtype=jnp.int32) * MOE_BLK
    block_e = jnp.minimum(jnp.sum((pad_end[None, :] <= block_start[:, None]).astype(jnp.int32), axis=1),
                          N_EXPERTS - 1)
    nact = (pad_end[-1] // MOE_BLK).astype(jnp.int32).reshape(1)
    return dest.astype(jnp.int32), slot_tok.reshape(nb, MOE_BLK), block_e, nact


def _encoder_layer(x, ada3, p):
    b, s, d = x.shape
    ck, sk = _rope_lane_tables(s)
    qna, kna, vna, q, k, v = _inproj(x, ada3, p, ck, sk)
    na_out = _na(qna, kna, vna, p['na_bias'])
    mla_out = _flash(q, k, v)
    x1, h2, e, g, rank, cnt = _post(na_out, mla_out, x, ada3, p)
    n = b * s
    e4 = e.reshape(n, LANE)[:, :TOP_K]
    rank4 = rank.reshape(n, LANE)[:, :TOP_K]
    counts = cnt[0, :N_EXPERTS].astype(jnp.int32)
    dest, slot_tok, block_e, nact = _moe_plan(e4, rank4, counts, n)
    nc = d // LANE
    y_sorted = _experts(h2.reshape(n * nc, LANE), slot_tok * nc, block_e, nact, p, nc)
    return _combine(dest, y_sorted, x1, g, ada3, p['ln_post_moe'])


def kernel(x_prompt, x_sample, c_prompt, c_sample, ln_pre_mix, ln_post_mix, ln_pre_moe, ln_post_moe, w_ada, b_ada, w_in, na_rpb, q_a_norm, w_q_b, kv_a_norm, w_kv_b, g_na_out, g_mla_out, w_o, w_router, b_router, w_mlp1, b_mlp1, w_mlp2, b_mlp2):
    depth = w_ada.shape[0]
    d = x_prompt.shape[-1]
    bp, bs = x_prompt.shape[0], x_sample.shape[0]
    rows = -(-(bp + bs) // 8) * 8
    y_prompt, y_sample = x_prompt, x_sample
    for l in range(depth):
        p = _prep_params(ln_pre_mix[l], ln_post_mix[l], ln_pre_moe[l], ln_post_moe[l], w_in[l], na_rpb[l],
                         q_a_norm[l], w_q_b[l], kv_a_norm[l], w_kv_b[l], g_na_out[l], g_mla_out[l], w_o[l],
                         w_router[l], b_router[l], w_mlp1[l], b_mlp1[l], w_mlp2[l], b_mlp2[l])
        c_all = jnp.concatenate([c_prompt, c_sample, jnp.zeros((rows - bp - bs, d), F32)], axis=0)
        ada3 = _ada(c_all, w_ada[l], b_ada[l]).reshape(rows, 6, d)
        y_prompt = _encoder_layer(y_prompt, ada3[:bp], p)
        y_sample = _encoder_layer(y_sample, ada3[bp:bp + bs], p)
    return (y_prompt, y_sample)
```

```python
import functools

import numpy as np
import jax
import jax.numpy as jnp
from jax import lax
from jax.experimental import pallas as pl
from jax.experimental.pallas import tpu as pltpu

F32 = jnp.float32
BF16 = jnp.bfloat16

GRID_W = 64
HEAD_DIM = 64
NA_HEADS = 8
NA_KR = 8
NA_KC = 16
MLA_HEADS = 8
MLA_NOPE = 64
MLA_ROPE = 32
MLA_V = 64
ROPE_THETA = 10000.0
N_EXPERTS = 32
TOP_K = 4
SWIGLU_LIMIT = 7.0
SWIGLU_ALPHA = 1.702
EPS = 1e-6

LANE = 128
HEAD_PAD = 128
NEG_BIG = -1e30
LOG2E = 1.4426950408889634

T_PROJ = 256
T_Q = 512
T_K = 512
NA_ROWS = 8
T_POST = 256
MOE_BLK = 512
T_COMB = 128
VMEM_LIMIT = 56 * 1024 * 1024


def _rms(x, n=None):
    n = x.shape[-1] if n is None else n
    return x * lax.rsqrt(jnp.sum(x * x, axis=-1, keepdims=True) * (1.0 / n) + EPS)


def _cparams(sem):
    return pltpu.CompilerParams(dimension_semantics=sem, vmem_limit_bytes=VMEM_LIMIT)


def _store_token_tiles(ref, x):
    rows, d = x.shape
    nc = d // LANE
    for c in range(nc):
        ref[pl.ds(c, rows, stride=nc), :] = x[:, c * LANE:(c + 1) * LANE]


def _load_token_tiles(ref, nc):
    rows = ref.shape[0] // nc
    return jnp.concatenate([ref[pl.ds(c, rows, stride=nc), :] for c in range(nc)], axis=1)


def _ada_kernel(c_ref, w_ref, b_ref, o_ref):
    c = c_ref[...]
    s = c / (1.0 + jnp.exp(-c))
    o_ref[...] = jnp.dot(s, w_ref[...], preferred_element_type=F32,
                         precision=lax.Precision.HIGHEST) + b_ref[...]


def _ada(c_all, w_ada, b_ada):
    rows, d = c_all.shape
    n_out = w_ada.shape[1]
    return pl.pallas_call(
        _ada_kernel,
        grid=(n_out // d,),
        in_specs=[pl.BlockSpec((rows, d), lambda j: (0, 0)),
                  pl.BlockSpec((d, d), lambda j: (0, j)),
                  pl.BlockSpec((1, d), lambda j: (0, j))],
        out_specs=pl.BlockSpec((rows, d), lambda j: (0, j)),
        out_shape=jax.ShapeDtypeStruct((rows, n_out), F32),
        compiler_params=_cparams(("arbitrary",)),
        name="ada",
    )(c_all, w_ada, b_ada.reshape(1, n_out))


def _inproj_kernel(x_ref, ada_ref, g_ref, ck_ref, sk_ref, w1_ref, qan_ref, wqa_ref, wqb_ref,
                   kvan_ref, wk_ref, wv_ref,
                   qna_ref, kna_ref, vna_ref, q_ref, k_ref, v_ref, *, na_w, q_lora, kv_lora, qscale):
    x = x_ref[0]
    sh = ada_ref[0, 0:1, :]
    sc = ada_ref[0, 1:2, :]
    h = (_rms(x) * g_ref[...]) * (1.0 + sc) + sh
    proj = jnp.dot(h.astype(BF16), w1_ref[...], preferred_element_type=F32)
    qna_ref[0] = (proj[:, 0:na_w] * (1.0 / float(np.sqrt(HEAD_DIM)))).astype(BF16)
    kna_ref[0] = proj[:, na_w:2 * na_w].astype(BF16)
    vna_ref[0] = proj[:, 2 * na_w:3 * na_w].astype(BF16)
    o = 3 * na_w
    cqn = (_rms(proj[:, o:o + q_lora]) * qan_ref[...]).astype(BF16)
    o += q_lora
    ckvn = (_rms(proj[:, o:o + kv_lora]) * kvan_ref[...]).astype(BF16)
    o += kv_lora
    kra = proj[:, o:o + HEAD_PAD]
    krb = proj[:, o + HEAD_PAD:o + 2 * HEAD_PAD]
    qa = jnp.dot(cqn, wqa_ref[...], preferred_element_type=F32)
    qb = jnp.dot(cqn, wqb_ref[...], preferred_element_type=F32)
    kn = jnp.dot(ckvn, wk_ref[...], preferred_element_type=F32)
    vv = jnp.dot(ckvn, wv_ref[...], preferred_element_type=F32)
    ck = ck_ref[...]
    sk = sk_ref[...]
    lane = lax.broadcasted_iota(jnp.int32, (1, HEAD_PAD), 1)
    nope = (lane < MLA_NOPE).astype(F32)
    ones_col = (lane == MLA_V).astype(F32)
    cq = (nope + ck) * qscale
    sq = sk * qscale
    kpe = kra * ck + krb * sk
    for hd in range(MLA_HEADS):
        sl = slice(hd * HEAD_PAD, (hd + 1) * HEAD_PAD)
        q_ref[0, :, sl] = (qa[:, sl] * cq + qb[:, sl] * sq).astype(BF16)
        k_ref[0, :, sl] = (kn[:, sl] + kpe).astype(BF16)
        v_ref[0, :, sl] = (vv[:, sl] + ones_col).astype(BF16)


def _inproj(x, ada3, p, ck, sk):
    b, s, d = x.shape
    t = min(T_PROJ, s)
    na_w = NA_HEADS * HEAD_DIM
    hw = MLA_HEADS * HEAD_PAD
    q_lora = p['wqa'].shape[0]
    kv_lora = p['wk'].shape[0]
    w1 = p['w1']
    kern = functools.partial(_inproj_kernel, na_w=na_w, q_lora=q_lora, kv_lora=kv_lora,
                             qscale=LOG2E / float(np.sqrt(MLA_NOPE + MLA_ROPE)))
    full = lambda a: pl.BlockSpec(a.shape, lambda i, j: (0,) * a.ndim)
    tok = lambda w: pl.BlockSpec((1, t, w), lambda i, j: (i, j, 0))
    outs = pl.pallas_call(
        kern,
        grid=(b, s // t),
        in_specs=[tok(d),
                  pl.BlockSpec((1, 6, d), lambda i, j: (i, 0, 0)),
                  full(p['ln_pre_mix']),
                  pl.BlockSpec((t, HEAD_PAD), lambda i, j: (j, 0)),
                  pl.BlockSpec((t, HEAD_PAD), lambda i, j: (j, 0)),
                  full(w1), full(p['q_a_norm']), full(p['wqa']), full(p['wqb']),
                  full(p['kv_a_norm']), full(p['wk']), full(p['wv'])],
        out_specs=[tok(na_w), tok(na_w), tok(na_w), tok(hw), tok(hw), tok(hw)],
        out_shape=[jax.ShapeDtypeStruct((b, s, na_w), BF16)] * 3
                  + [jax.ShapeDtypeStruct((b, s, hw), BF16)] * 3,
        compiler_params=_cparams(("arbitrary", "arbitrary")),
        name="inproj",
    )(x, ada3, p['ln_pre_mix'], ck, sk, w1, p['q_a_norm'], p['wqa'], p['wqb'],
      p['kv_a_norm'], p['wk'], p['wv'])
    return outs


def _na_kernel(q_ref, kp_ref, kc_ref, kn_ref, vp_ref, vc_ref, vn_ref, bias_ref, o_ref,
               kcat, vcat, *, rows):
    blk = pl.program_id(1)
    nb = NA_ROWS * GRID_W
    win = NA_KR * GRID_W
    for i, (kr, vr) in enumerate(((kp_ref, vp_ref), (kc_ref, vc_ref), (kn_ref, vn_ref))):
        kcat[i * nb:(i + 1) * nb, :] = kr[0]
        vcat[i * nb:(i + 1) * nb, :] = vr[0]
    lane = lax.broadcasted_iota(jnp.int32, (GRID_W, LANE), 1)
    lo = lane < HEAD_DIM

    def row_body(rl, carry):
        r = blk * NA_ROWS + rl
        rs = jnp.clip(r - NA_KR // 2, 0, rows - NA_KR)
        d = r - rs
        off = pl.multiple_of((rs - (blk - 1) * NA_ROWS) * GRID_W, GRID_W)
        qoff = pl.multiple_of(rl * GRID_W, GRID_W)
        scores = []
        for hp in range(NA_HEADS // 2):
            sl = slice(hp * LANE, (hp + 1) * LANE)
            qp = q_ref[0, pl.ds(qoff, GRID_W), sl]
            kw = kcat[pl.ds(off, win), sl]
            zero = jnp.zeros_like(qp)
            qm = jnp.concatenate([jnp.where(lo, qp, zero), jnp.where(lo, zero, qp)], axis=0)
            s = lax.dot_general(qm, kw, (((1,), (1,)), ((), ())), preferred_element_type=F32)
            scores.append(s + bias_ref[d, hp])
        probs, inv_l = [], []
        for s in scores:
            m = jnp.max(s, axis=-1, keepdims=True)
            e = jnp.exp(s - m)
            inv_l.append(1.0 / jnp.sum(e, axis=-1, keepdims=True))
            probs.append(e.astype(BF16))
        pair_outs = []
        for hp in range(NA_HEADS // 2):
            vw = vcat[pl.ds(off, win), hp * LANE:(hp + 1) * LANE]
            o2 = jnp.dot(probs[hp], vw, preferred_element_type=F32) * inv_l[hp]
            pair_outs.append(jnp.where(lo, o2[:GRID_W], o2[GRID_W:]))
        o_ref[0, pl.ds(qoff, GRID_W), :] = jnp.concatenate(pair_outs, axis=1)
        return carry

    lax.fori_loop(0, NA_ROWS, row_body, 0)


def _na_bias_table(rpb):
    nh, nrow, nrel = rpb.shape
    cols = np.arange(GRID_W)
    cstart = np.clip(cols - NA_KC // 2, 0, GRID_W - NA_KC)
    j = np.arange(GRID_W)
    inwin = (j[None, :] >= cstart[:, None]) & (j[None, :] < cstart[:, None] + NA_KC)
    period = 2 * GRID_W - 1
    vpad = jnp.concatenate([rpb[..., NA_KC - 1:], jnp.zeros((nh, nrow, period - nrel), rpb.dtype),
                            rpb[..., :NA_KC - 1]], axis=-1)
    toep = jnp.tile(vpad, (1, 1, GRID_W))[..., :GRID_W * (period - 1)]
    toep = toep.reshape(nh, nrow, GRID_W, period - 1)[..., :GRID_W]
    tabs = []
    for d in range(NA_KR):
        t = toep[:, NA_KR - 1 - d:2 * NA_KR - 1 - d]
        tabs.append(t.transpose(0, 2, 1, 3))
    tab = jnp.stack(tabs, axis=0).astype(F32)
    tab = jnp.where(jnp.asarray(inwin)[None, None, :, None, :], tab, NEG_BIG)
    return tab.reshape(NA_KR, nh // 2, 2 * GRID_W, NA_KR * GRID_W)


def _na(q, k, v, bias):
    b, s, w = q.shape
    rows = s // GRID_W
    assert rows % NA_ROWS == 0 and rows >= 2 * NA_ROWS
    nblk = rows // NA_ROWS
    nb = NA_ROWS * GRID_W
    cur = pl.BlockSpec((1, nb, w), lambda i, j: (i, j, 0))
    prv = pl.BlockSpec((1, nb, w), lambda i, j: (i, jnp.maximum(j - 1, 0), 0))
    nxt = pl.BlockSpec((1, nb, w), lambda i, j: (i, jnp.minimum(j + 1, nblk - 1), 0))
    return pl.pallas_call(
        functools.partial(_na_kernel, rows=rows),
        grid=(b, nblk),
        in_specs=[cur, prv, cur, nxt, prv, cur, nxt,
                  pl.BlockSpec(bias.shape, lambda i, j: (0, 0, 0, 0))],
        out_specs=cur,
        out_shape=jax.ShapeDtypeStruct((b, s, w), F32),
        scratch_shapes=[pltpu.VMEM((3 * nb, w), BF16), pltpu.VMEM((3 * nb, w), BF16)],
        compiler_params=_cparams(("arbitrary", "arbitrary")),
        name="na",
    )(q, k, k, k, v, v, v, bias)


def _flash_kernel(q_ref, k_ref, v_ref, o_ref, m_sc, acc_sc, s_sc, *, tk, nk, unroll):
    q = q_ref[0]
    m_sc[...] = jnp.full(m_sc.shape, -jnp.inf, F32)
    acc_sc[...] = jnp.zeros(acc_sc.shape, F32)

    def scores(j):
        off = pl.multiple_of(j * tk, tk)
        ks = k_ref[0, pl.ds(off, tk), :]
        return lax.dot_general(q, ks, (((1,), (1,)), ((), ())), preferred_element_type=F32)

    def accumulate(s, j):
        off = pl.multiple_of(j * tk, tk)
        vs = v_ref[0, pl.ds(off, tk), :]
        cols = [s[:, c * LANE:(c + 1) * LANE] for c in range(tk // LANE)]
        cm = cols[0]
        for sc in cols[1:]:
            cm = jnp.maximum(cm, sc)
        m_prev = m_sc[...]
        m_new = jnp.maximum(m_prev, jnp.max(cm, axis=-1, keepdims=True))
        alpha = jnp.exp2(m_prev - m_new)
        p = jnp.concatenate([jnp.exp2((sc - m_new).astype(BF16)) for sc in cols], axis=1)
        acc_sc[...] = alpha * acc_sc[...] + jnp.dot(p, vs, preferred_element_type=F32)
        m_sc[...] = m_new

    s_sc[...] = scores(0)

    def trip(i, carry):
        s_cur = s_sc[...]
        for u in range(unroll):
            j = i * unroll + u
            s_next = scores(jnp.minimum(j + 1, nk - 1))
            accumulate(s_cur, j)
            s_cur = s_next
        s_sc[...] = s_cur
        return carry

    lax.fori_loop(0, nk // unroll, trip, 0)
    acc = acc_sc[...]
    o_ref[0] = acc / acc[:, MLA_V:MLA_V + 1]


def _flash(q, k, v):
    b, s, hw = q.shape
    nh = hw // HEAD_PAD
    tq = min(T_Q, s)
    tk = min(T_K, s)
    nk = s // tk
    unroll = 4 if nk % 4 == 0 else (2 if nk % 2 == 0 else 1)
    return pl.pallas_call(
        functools.partial(_flash_kernel, tk=tk, nk=nk, unroll=unroll),
        grid=(b, nh, s // tq),
        in_specs=[pl.BlockSpec((1, tq, HEAD_PAD), lambda i, h, j: (i, j, h)),
                  pl.BlockSpec((1, s, HEAD_PAD), lambda i, h, j: (i, 0, h)),
                  pl.BlockSpec((1, s, HEAD_PAD), lambda i, h, j: (i, 0, h))],
        out_specs=pl.BlockSpec((1, tq, HEAD_PAD), lambda i, h, j: (i, j, h)),
        out_shape=jax.ShapeDtypeStruct((b, s, hw), F32),
        scratch_shapes=[pltpu.VMEM((tq, HEAD_PAD), F32), pltpu.VMEM((tq, HEAD_PAD), F32),
                        pltpu.VMEM((tq, tk), F32)],
        compiler_params=_cparams(("arbitrary", "arbitrary", "arbitrary")),
        name="flash",
    )(q, k, v)


def _post_kernel(na_ref, mla_ref, x_ref, ada_ref, gna_ref, gmla_ref, wona_ref, womla_ref,
                 lnpost_ref, lnpre_ref, wr_ref, br_ref,
                 x1_ref, h2_ref, e_ref, g_ref, rank_ref, cnt_ref, base_sc):
    first = jnp.logical_and(pl.program_id(0) == 0, pl.program_id(1) == 0)

    @pl.when(first)
    def _():
        base_sc[...] = jnp.zeros(base_sc.shape, F32)

    t = x_ref.shape[1]
    na = na_ref[0]
    nan_ = _rms(na) * gna_ref[...]
    ml = mla_ref[0]
    lane_w = lax.broadcasted_iota(jnp.int32, ml.shape, 1)
    ml = jnp.where((lane_w % HEAD_PAD) < MLA_V, ml, 0.0)
    mln = _rms(ml, n=MLA_HEADS * MLA_V) * gmla_ref[...]
    mix = (jnp.dot(nan_.astype(BF16), wona_ref[...], preferred_element_type=F32)
           + jnp.dot(mln.astype(BF16), womla_ref[...], preferred_element_type=F32))
    g_a = ada_ref[0, 2:3, :]
    sh_m = ada_ref[0, 3:4, :]
    sc_m = ada_ref[0, 4:5, :]
    x1 = x_ref[0] + g_a * (_rms(mix) * lnpost_ref[...])
    x1_ref[0] = x1
    h2 = (_rms(x1) * lnpre_ref[...]) * (1.0 + sc_m) + sh_m
    _store_token_tiles(h2_ref.at[0], h2)

    h_hi = h2.astype(BF16)
    h_lo = (h2 - h_hi.astype(F32)).astype(BF16)
    logits = jnp.dot(jnp.concatenate([h_hi, h_lo, h_hi], axis=1), wr_ref[...],
                     preferred_element_type=F32) + br_ref[...]
    lane = lax.broadcasted_iota(jnp.int32, (t, LANE), 1)
    lane_f = lane.astype(F32)
    work = jnp.where(lane < N_EXPERTS, logits, -jnp.inf)
    vals, idxs = [], []
    onehot = jnp.zeros((t, LANE), F32)
    for _ in range(TOP_K):
        mk = jnp.max(work, axis=-1, keepdims=True)
        ik = jnp.min(jnp.where(work == mk, lane_f, float(LANE)), axis=-1, keepdims=True)
        sel = lane_f == ik
        work = jnp.where(sel, -jnp.inf, work)
        onehot = jnp.where(sel, 1.0, onehot)
        vals.append(mk)
        idxs.append(ik)
    exps = [jnp.exp(vk - vals[0]) for vk in vals]
    denom = exps[0]
    for ek in exps[1:]:
        denom = denom + ek

    r_i = lax.broadcasted_iota(jnp.int32, (t, t), 0)
    c_i = lax.broadcasted_iota(jnp.int32, (t, t), 1)
    tril = jnp.where(c_i < r_i, 1.0, 0.0).astype(BF16)
    pref = jnp.dot(tril, onehot.astype(BF16), preferred_element_type=F32) + base_sc[...]

    e_out = jnp.zeros((t, LANE), F32)
    g_out = jnp.zeros((t, LANE), F32)
    r_out = jnp.zeros((t, LANE), F32)
    for kk in range(TOP_K):
        rk = jnp.sum(jnp.where(lane_f == idxs[kk], pref, 0.0), axis=-1, keepdims=True)
        here = lane == kk
        e_out = jnp.where(here, idxs[kk], e_out)
        g_out = jnp.where(here, exps[kk] / denom, g_out)
        r_out = jnp.where(here, rk, r_out)
    e_ref[0] = e_out.astype(jnp.int32)
    g_ref[0] = g_out
    rank_ref[0] = r_out.astype(jnp.int32)
    base_sc[...] = base_sc[...] + jnp.sum(onehot, axis=0, keepdims=True)
    cnt_ref[...] = jnp.broadcast_to(base_sc[...], cnt_ref.shape)


def _post(na_out, mla_out, x, ada3, p):
    b, s, d = x.shape
    t = min(T_POST, s)
    full = lambda a: pl.BlockSpec(a.shape, lambda i, j: (0,) * a.ndim)
    tok = lambda w: pl.BlockSpec((1, t, w), lambda i, j: (i, j, 0))
    return pl.pallas_call(
        _post_kernel,
        grid=(b, s // t),
        in_specs=[tok(na_out.shape[-1]), tok(mla_out.shape[-1]), tok(d),
                  pl.BlockSpec((1, 6, d), lambda i, j: (i, 0, 0)),
                  full(p['g_na_out']), full(p['g_mla_pad']), full(p['wo_na']), full(p['wo_mla']),
                  full(p['ln_post_mix']), full(p['ln_pre_moe']), full(p['w_router']), full(p['b_router'])],
        out_specs=[tok(d), pl.BlockSpec((1, t * (d // LANE), LANE), lambda i, j: (i, j, 0)),
                   tok(LANE), tok(LANE), tok(LANE),
                   pl.BlockSpec((8, LANE), lambda i, j: (0, 0))],
        out_shape=[jax.ShapeDtypeStruct((b, s, d), F32), jax.ShapeDtypeStruct((b, s * (d // LANE), LANE), F32),
                   jax.ShapeDtypeStruct((b, s, LANE), jnp.int32), jax.ShapeDtypeStruct((b, s, LANE), F32),
                   jax.ShapeDtypeStruct((b, s, LANE), jnp.int32), jax.ShapeDtypeStruct((8, LANE), F32)],
        scratch_shapes=[pltpu.VMEM((1, LANE), F32)],
        compiler_params=_cparams(("arbitrary", "arbitrary")),
        name="post",
    )(na_out, mla_out, x, ada3, p['g_na_out'], p['g_mla_pad'], p['wo_na'], p['wo_mla'],
      p['ln_post_mix'], p['ln_pre_moe'], p['w_router'], p['b_router'])


def _experts_kernel(be_ref, nact_ref, tok_hbm, h_hbm, w1g_ref, w1l_ref, w2_ref, b1g_ref, b1l_ref, b2_ref,
                    y_ref, idx_sm, xbuf, isem, rsem, *, blk, nc):
    i = pl.program_id(0)
    nact = nact_ref[0]

    def idx_copy(step, slot):
        return pltpu.make_async_copy(tok_hbm.at[step], idx_sm.at[slot], isem.at[slot])

    def issue_rows(slot):
        def one(j, carry):
            src = pl.multiple_of(idx_sm[slot, j], nc)
            dst = pl.multiple_of(j * nc, nc)
            pltpu.make_async_copy(h_hbm.at[pl.ds(src, nc), :], xbuf.at[slot, pl.ds(dst, nc), :],
                                  rsem.at[slot]).start()
            return carry
        lax.fori_loop(0, blk, one, 0, unroll=16)

    def wait_rows(slot):
        pltpu.make_async_copy(h_hbm.at[pl.ds(0, blk * nc), :], xbuf.at[slot], rsem.at[slot]).wait()

    @pl.when(jnp.logical_and(i == 0, nact > 0))
    def _():
        idx_copy(0, 0).start()
        idx_copy(0, 0).wait()
        issue_rows(0)

        @pl.when(nact > 1)
        def _():
            idx_copy(1, 1).start()

    cur = lax.rem(i, 2)
    nxt = 1 - cur

    has_next = i + 1 < nact

    @pl.when(has_next)
    def _():
        idx_copy(i + 1, nxt).wait()

    @pl.when(i < nact)
    def _():
        wait_rows(cur)
        for j in range(blk):
            @pl.when(has_next)
            def _():
                src = pl.multiple_of(idx_sm[nxt, j], nc)
                pltpu.make_async_copy(h_hbm.at[pl.ds(src, nc), :], xbuf.at[nxt, pl.ds(j * nc, nc), :],
                                      rsem.at[nxt]).start()
        x = _load_token_tiles(xbuf.at[cur], nc).astype(BF16)
        ug = jnp.dot(x, w1g_ref[0], preferred_element_type=F32) + b1g_ref[0]
        ul = jnp.dot(x, w1l_ref[0], preferred_element_type=F32) + b1l_ref[0]
        x_glu = jnp.minimum(ug, SWIGLU_LIMIT)
        x_lin = jnp.clip(ul, -SWIGLU_LIMIT, SWIGLU_LIMIT)
        act = (x_lin + 1.0) * (x_glu * (1.0 / (1.0 + jnp.exp(-SWIGLU_ALPHA * x_glu))))
        y = jnp.dot(act.astype(BF16), w2_ref[0], preferred_element_type=F32) + b2_ref[0]
        _store_token_tiles(y_ref, y)

    @pl.when(i + 2 < nact)
    def _():
        idx_copy(i + 2, cur).start()

    @pl.when(i >= nact)
    def _():
        y_ref[...] = jnp.zeros(y_ref.shape, F32)


def _experts(h2, slot_tok, block_e, nact, p, nc):
    nb, blk = slot_tok.shape
    wspec = lambda a: pl.BlockSpec((1,) + a.shape[1:], lambda i, be, na: (be[i], 0, 0))
    grid_spec = pltpu.PrefetchScalarGridSpec(
        num_scalar_prefetch=2,
        grid=(nb,),
        in_specs=[pl.BlockSpec(memory_space=pl.ANY), pl.BlockSpec(memory_space=pl.ANY),
                  wspec(p['w1g']), wspec(p['w1l']), wspec(p['w2']),
                  wspec(p['b1g']), wspec(p['b1l']), wspec(p['b2'])],
        out_specs=pl.BlockSpec((blk * nc, LANE), lambda i, be, na: (i, 0)),
        scratch_shapes=[pltpu.SMEM((2, blk), jnp.int32), pltpu.VMEM((2, blk * nc, LANE), F32),
                        pltpu.SemaphoreType.DMA((2,)), pltpu.SemaphoreType.DMA((2,))],
    )
    return pl.pallas_call(
        functools.partial(_experts_kernel, blk=blk, nc=nc),
        grid_spec=grid_spec,
        out_shape=jax.ShapeDtypeStruct((nb * blk * nc, LANE), F32),
        compiler_params=_cparams(("arbitrary",)),
        name="experts",
    )(block_e, nact, slot_tok, h2, p['w1g'], p['w1l'], p['w2'], p['b1g'], p['b1l'], p['b2'])


def _deint_kernel(w_ref, sel_ref, g_ref, l_ref):
    sel = sel_ref[...]
    for blk in range(w_ref.shape[2] // (2 * LANE)):
        w = w_ref[0, :, blk * 2 * LANE:(blk + 1) * 2 * LANE].astype(BF16)
        r = jnp.dot(w, sel, preferred_element_type=F32).astype(BF16)
        g_ref[0, :, blk * LANE:(blk + 1) * LANE] = r[:, :LANE]
        l_ref[0, :, blk * LANE:(blk + 1) * LANE] = r[:, LANE:]


def _deinterleave_w1(w_mlp1):
    ne, d, f2 = w_mlp1.shape
    f = f2 // 2
    rows = min(512, d)
    src = lax.broadcasted_iota(jnp.int32, (2 * LANE, 2 * LANE), 0)
    dst = lax.broadcasted_iota(jnp.int32, (2 * LANE, 2 * LANE), 1)
    sel = jnp.where(dst < LANE, src == 2 * dst, src == 2 * (dst - LANE) + 1).astype(BF16)
    return pl.pallas_call(
        _deint_kernel,
        grid=(ne, d // rows),
        in_specs=[pl.BlockSpec((1, rows, f2), lambda e, r: (e, r, 0)),
                  pl.BlockSpec((2 * LANE, 2 * LANE), lambda e, r: (0, 0))],
        out_specs=[pl.BlockSpec((1, rows, f), lambda e, r: (e, r, 0))] * 2,
        out_shape=[jax.ShapeDtypeStruct((ne, d, f), BF16)] * 2,
        compiler_params=_cparams(("arbitrary", "arbitrary")),
        name="deint",
    )(w_mlp1, sel)


def _combine_kernel(dest_hbm, y_hbm, x1_ref, g_ref, ada_ref, ln_ref, o_ref,
                    idx_sm, ybuf, isem, rsem, *, t, nsteps, nc):
    i = pl.program_id(0) * pl.num_programs(1) + pl.program_id(1)

    def idx_copy(step, slot):
        return pltpu.make_async_copy(dest_hbm.at[step], idx_sm.at[slot], isem.at[slot])

    def issue_rows(slot):
        def one(j, carry):
            for kk in range(TOP_K):
                src = pl.multiple_of(idx_sm[slot, j * TOP_K + kk], nc)
                dst = pl.multiple_of(j * nc, nc)
                pltpu.make_async_copy(y_hbm.at[pl.ds(src, nc), :], ybuf.at[slot, kk, pl.ds(dst, nc), :],
                                      rsem.at[slot]).start()
            return carry
        lax.fori_loop(0, t, one, 0, unroll=4)

    def wait_rows(slot):
        for kk in range(TOP_K):
            pltpu.make_async_copy(y_hbm.at[pl.ds(0, t * nc), :], ybuf.at[slot, kk], rsem.at[slot]).wait()

    @pl.when(i == 0)
    def _():
        idx_copy(0, 0).start()
        idx_copy(0, 0).wait()
        issue_rows(0)
        if nsteps > 1:
            idx_copy(1, 1).start()

    cur = lax.rem(i, 2)
    nxt = 1 - cur

    has_next = i + 1 < nsteps

    @pl.when(has_next)
    def _():
        idx_copy(i + 1, nxt).wait()

    for j in range(t):
        for kk in range(TOP_K):
            @pl.when(has_next)
            def _():
                src = pl.multiple_of(idx_sm[nxt, j * TOP_K + kk], nc)
                pltpu.make_async_copy(y_hbm.at[pl.ds(src, nc), :], ybuf.at[nxt, kk, pl.ds(j * nc, nc), :],
                                      rsem.at[nxt]).start()

    @pl.when(i + 2 < nsteps)
    def _():
        idx_copy(i + 2, cur).start()

    wait_rows(cur)
    g = g_ref[0]
    y = jnp.zeros((t, x1_ref.shape[2]), F32)
    for kk in range(TOP_K):
        y = y + _load_token_tiles(ybuf.at[cur, kk], nc) * g[:, kk:kk + 1]
    g_m = ada_ref[0, 5:6, :]
    o_ref[0] = x1_ref[0] + g_m * (_rms(y) * ln_ref[...])


def _combine(dest, y_sorted, x1, gates, ada3, ln_post_moe):
    b, s, d = x1.shape
    t = min(T_COMB, s)
    ns = s // t
    nsteps = b * ns
    nc = d // LANE
    dest2 = (dest * nc).reshape(nsteps, t * TOP_K)
    return pl.pallas_call(
        functools.partial(_combine_kernel, t=t, nsteps=nsteps, nc=nc),
        grid=(b, ns),
        in_specs=[pl.BlockSpec(memory_space=pl.ANY), pl.BlockSpec(memory_space=pl.ANY),
                  pl.BlockSpec((1, t, d), lambda i, j: (i, j, 0)),
                  pl.BlockSpec((1, t, LANE), lambda i, j: (i, j, 0)),
                  pl.BlockSpec((1, 6, d), lambda i, j: (i, 0, 0)),
                  pl.BlockSpec(ln_post_moe.shape, lambda i, j: (0, 0))],
        out_specs=pl.BlockSpec((1, t, d), lambda i, j: (i, j, 0)),
        out_shape=jax.ShapeDtypeStruct((b, s, d), F32),
        scratch_shapes=[pltpu.SMEM((2, t * TOP_K), jnp.int32),
                        pltpu.VMEM((2, TOP_K, t * nc, LANE), F32),
                        pltpu.SemaphoreType.DMA((2,)), pltpu.SemaphoreType.DMA((2,))],
        compiler_params=_cparams(("arbitrary", "arbitrary")),
        name="combine",
    )(dest2, y_sorted, x1, gates, ada3, ln_post_moe)


def _rope_partner(w):
    half = MLA_ROPE // 2
    return jnp.concatenate([-w[..., half:], w[..., :half]], axis=-1)


def _prep_params(ln_pre_mix, ln_post_mix, ln_pre_moe, ln_post_moe, w_in, na_rpb, q_a_norm, w_q_b,
                 kv_a_norm, w_kv_b, g_na_out, g_mla_out, w_o, w_router, b_router,
                 w_mlp1, b_mlp1, w_mlp2, b_mlp2):
    d = w_in.shape[0]
    na_w = NA_HEADS * HEAD_DIM
    q_lora = w_q_b.shape[0]
    kv_lora = w_kv_b.shape[0]
    qk = MLA_NOPE + MLA_ROPE
    main_w = 3 * na_w + q_lora + kv_lora
    w_kr = w_in[:, main_w:main_w + MLA_ROPE]
    zpad = lambda n: jnp.zeros((d, n), F32)
    kra = jnp.concatenate([zpad(MLA_NOPE), w_kr, zpad(HEAD_PAD - qk)], axis=1)
    krb = jnp.concatenate([zpad(MLA_NOPE), _rope_partner(w_kr), zpad(HEAD_PAD - qk)], axis=1)
    w1 = jnp.concatenate([w_in[:, :main_w], kra, krb], axis=1).astype(BF16)

    wq = w_q_b.reshape(q_lora, MLA_HEADS, qk)
    zq = lambda n: jnp.zeros((q_lora, MLA_HEADS, n), F32)
    wqa = jnp.concatenate([wq, zq(HEAD_PAD - qk)], axis=2).reshape(q_lora, -1).astype(BF16)
    wqb = jnp.concatenate([zq(MLA_NOPE), _rope_partner(wq[:, :, MLA_NOPE:]), zq(HEAD_PAD - qk)],
                          axis=2).reshape(q_lora, -1).astype(BF16)
    wkv = w_kv_b.reshape(kv_lora, MLA_HEADS, MLA_NOPE + MLA_V)
    zk = lambda n: jnp.zeros((kv_lora, MLA_HEADS, n), F32)
    wk = jnp.concatenate([wkv[:, :, :MLA_NOPE], zk(HEAD_PAD - MLA_NOPE)], axis=2).reshape(kv_lora, -1).astype(BF16)
    wv = jnp.concatenate([wkv[:, :, MLA_NOPE:], zk(HEAD_PAD - MLA_V)], axis=2).reshape(kv_lora, -1).astype(BF16)

    mla_w = MLA_HEADS * MLA_V
    g_mla_pad = jnp.concatenate([g_mla_out.reshape(MLA_HEADS, MLA_V),
                                 jnp.zeros((MLA_HEADS, HEAD_PAD - MLA_V), F32)], axis=1).reshape(1, -1)
    wo_mla = w_o[na_w:na_w + mla_w].reshape(MLA_HEADS, MLA_V, d)
    wo_mla = jnp.concatenate([wo_mla, jnp.zeros((MLA_HEADS, HEAD_PAD - MLA_V, d), F32)],
                             axis=1).reshape(MLA_HEADS * HEAD_PAD, d).astype(BF16)
    w_r = jnp.concatenate([w_router, jnp.zeros((d, LANE - N_EXPERTS), F32)], axis=1)
    w_r_hi = w_r.astype(BF16)
    w_r_lo = (w_r - w_r_hi.astype(F32)).astype(BF16)
    w_r = jnp.concatenate([w_r_hi, w_r_hi, w_r_lo], axis=0)
    b_r = jnp.concatenate([b_router, jnp.zeros((LANE - N_EXPERTS,), F32)]).reshape(1, LANE)
    ne = w_mlp1.shape[0]
    w1g, w1l = _deinterleave_w1(w_mlp1)
    return {
        'ln_pre_mix': ln_pre_mix.reshape(1, d), 'ln_post_mix': ln_post_mix.reshape(1, d),
        'ln_pre_moe': ln_pre_moe.reshape(1, d), 'ln_post_moe': ln_post_moe.reshape(1, d),
        'w1': w1, 'q_a_norm': q_a_norm.reshape(1, q_lora), 'wqa': wqa, 'wqb': wqb,
        'kv_a_norm': kv_a_norm.reshape(1, kv_lora), 'wk': wk, 'wv': wv,
        'na_bias': _na_bias_table(na_rpb),
        'g_na_out': g_na_out.reshape(1, na_w), 'g_mla_pad': g_mla_pad,
        'wo_na': w_o[:na_w].astype(BF16), 'wo_mla': wo_mla,
        'w_router': w_r, 'b_router': b_r,
        'w1g': w1g, 'w1l': w1l,
        'w2': w_mlp2.astype(BF16),
        'b1g': b_mlp1[:, 0::2].reshape(ne, 1, -1), 'b1l': b_mlp1[:, 1::2].reshape(ne, 1, -1),
        'b2': b_mlp2.reshape(ne, 1, -1),
    }


def _rope_lane_tables(s):
    half = MLA_ROPE // 2
    inv = ROPE_THETA ** (-jnp.arange(half, dtype=F32) / half)
    ang = jnp.arange(s, dtype=F32)[:, None] * inv[None, :]
    cos, sin = jnp.cos(ang), jnp.sin(ang)
    z = lambda n: jnp.zeros((s, n), F32)
    tail = HEAD_PAD - MLA_NOPE - MLA_ROPE
    ck = jnp.concatenate([z(MLA_NOPE), cos, cos, z(tail)], axis=1)
    sk = jnp.concatenate([z(MLA_NOPE), sin, sin, z(tail)], axis=1)
    return ck, sk


def _moe_plan(e4, rank4, counts, n, nc):
    nk = n * TOP_K
    padded = (counts + MOE_BLK - 1) // MOE_BLK * MOE_BLK
    pad_end = jnp.cumsum(padded)
    pad_start = pad_end - padded
    dest = pad_start[e4] + rank4
    nb = (nk + N_EXPERTS * (MOE_BLK - 1) + MOE_BLK - 1) // MOE_BLK
    tok = jnp.broadcast_to(jnp.arange(n, dtype=jnp.int32)[:, None] * nc, (n, TOP_K))
    slot_tok = jnp.zeros((nb * MOE_BLK,), jnp.int32).at[dest.reshape(-1)].set(
        tok.reshape(-1), unique_indices=True, mode='promise_in_bounds')
    block_start = jnp.arange(nb, dtype=jnp.int32) * MOE_BLK
    block_e = jnp.minimum(jnp.sum((pad_end[None, :] <= block_start[:, None]).astype(jnp.int32), axis=1),
                          N_EXPERTS - 1)
    nact = (pad_end[-1] // MOE_BLK).astype(jnp.int32).reshape(1)
    return dest.astype(jnp.int32), slot_tok.reshape(nb, MOE_BLK), block_e, nact


def _encoder_layer(x, ada3, p):
    b, s, d = x.shape
    ck, sk = _rope_lane_tables(s)
    qna, kna, vna, q, k, v = _inproj(x, ada3, p, ck, sk)
    na_out = _na(qna, kna, vna, p['na_bias'])
    mla_out = _flash(q, k, v)
    x1, h2, e, g, rank, cnt = _post(na_out, mla_out, x, ada3, p)
    n = b * s
    e4 = e.reshape(n, LANE)[:, :TOP_K]
    rank4 = rank.reshape(n, LANE)[:, :TOP_K]
    counts = cnt[0, :N_EXPERTS].astype(jnp.int32)
    nc = d // LANE
    dest, slot_tok, block_e, nact = _moe_plan(e4, rank4, counts, n, nc)
    y_sorted = _experts(h2.reshape(n * nc, LANE), slot_tok, block_e, nact, p, nc)
    return _combine(dest, y_sorted, x1, g, ada3, p['ln_post_moe'])


def kernel(x_prompt, x_sample, c_prompt, c_sample, ln_pre_mix, ln_post_mix, ln_pre_moe, ln_post_moe, w_ada, b_ada, w_in, na_rpb, q_a_norm, w_q_b, kv_a_norm, w_kv_b, g_na_out, g_mla_out, w_o, w_router, b_router, w_mlp1, b_mlp1, w_mlp2, b_mlp2):
    depth = w_ada.shape[0]
    d = x_prompt.shape[-1]
    bp, bs = x_prompt.shape[0], x_sample.shape[0]
    rows = -(-(bp + bs) // 8) * 8
    y_prompt, y_sample = x_prompt, x_sample
    for l in range(depth):
        p = _prep_params(ln_pre_mix[l], ln_post_mix[l], ln_pre_moe[l], ln_post_moe[l], w_in[l], na_rpb[l],
                         q_a_norm[l], w_q_b[l], kv_a_norm[l], w_kv_b[l], g_na_out[l], g_mla_out[l], w_o[l],
                         w_router[l], b_router[l], w_mlp1[l], b_mlp1[l], w_mlp2[l], b_mlp2[l])
        c_all = jnp.concatenate([c_prompt, c_sample, jnp.zeros((rows - bp - bs, d), F32)], axis=0)
        ada3 = _ada(c_all, w_ada[l], b_ada[l]).reshape(rows, 6, d)
        y_prompt = _encoder_layer(y_prompt, ada3[:bp], p)
        y_sample = _encoder_layer(y_sample, ada3[bp:bp + bs], p)
    return (y_prompt, y_sample)
```

```python
import functools

import numpy as np
import jax
import jax.numpy as jnp
from jax import lax
from jax.experimental import pallas as pl
from jax.experimental.pallas import tpu as pltpu

F32 = jnp.float32
BF16 = jnp.bfloat16

GRID_W = 64
HEAD_DIM = 64
NA_HEADS = 8
NA_KR = 8
NA_KC = 16
MLA_HEADS = 8
MLA_NOPE = 64
MLA_ROPE = 32
MLA_V = 64
ROPE_THETA = 10000.0
N_EXPERTS = 32
TOP_K = 4
SWIGLU_LIMIT = 7.0
SWIGLU_ALPHA = 1.702
EPS = 1e-6

LANE = 128
HEAD_PAD = 128
NEG_BIG = -1e30
LOG2E = 1.4426950408889634

T_PROJ = 256
T_Q = 512
T_K = 512
NA_ROWS = 8
T_POST = 256
MOE_BLK = 512
T_COMB = 128
VMEM_LIMIT = 56 * 1024 * 1024


def _rms(x, n=None):
    n = x.shape[-1] if n is None else n
    return x * lax.rsqrt(jnp.sum(x * x, axis=-1, keepdims=True) * (1.0 / n) + EPS)


def _cparams(sem):
    return pltpu.CompilerParams(dimension_semantics=sem, vmem_limit_bytes=VMEM_LIMIT)


def _store_token_tiles(ref, x):
    rows, d = x.shape
    nc = d // LANE
    for c in range(nc):
        ref[pl.ds(c, rows, stride=nc), :] = x[:, c * LANE:(c + 1) * LANE]


def _load_token_tiles(ref, nc):
    rows = ref.shape[0] // nc
    return jnp.concatenate([ref[pl.ds(c, rows, stride=nc), :] for c in range(nc)], axis=1)


def _ada_kernel(c_ref, w_ref, b_ref, o_ref):
    c = c_ref[...]
    s = c / (1.0 + jnp.exp(-c))
    o_ref[...] = jnp.dot(s, w_ref[...], preferred_element_type=F32,
                         precision=lax.Precision.HIGHEST) + b_ref[...]


def _ada(c_all, w_ada, b_ada):
    rows, d = c_all.shape
    n_out = w_ada.shape[1]
    return pl.pallas_call(
        _ada_kernel,
        grid=(n_out // d,),
        in_specs=[pl.BlockSpec((rows, d), lambda j: (0, 0)),
                  pl.BlockSpec((d, d), lambda j: (0, j)),
                  pl.BlockSpec((1, d), lambda j: (0, j))],
        out_specs=pl.BlockSpec((rows, d), lambda j: (0, j)),
        out_shape=jax.ShapeDtypeStruct((rows, n_out), F32),
        compiler_params=_cparams(("arbitrary",)),
        name="ada",
    )(c_all, w_ada, b_ada.reshape(1, n_out))


def _inproj_kernel(x_ref, ada_ref, g_ref, ck_ref, sk_ref, w1_ref, qan_ref, wqa_ref, wqb_ref,
                   kvan_ref, wk_ref, wv_ref,
                   qna_ref, kna_ref, vna_ref, q_ref, k_ref, v_ref, *, na_w, q_lora, kv_lora, qscale):
    x = x_ref[0]
    sh = ada_ref[0, 0:1, :]
    sc = ada_ref[0, 1:2, :]
    h = (_rms(x) * g_ref[...]) * (1.0 + sc) + sh
    proj = jnp.dot(h.astype(BF16), w1_ref[...], preferred_element_type=F32)
    qna_ref[0] = (proj[:, 0:na_w] * (1.0 / float(np.sqrt(HEAD_DIM)))).astype(BF16)
    kna_ref[0] = proj[:, na_w:2 * na_w].astype(BF16)
    vna_ref[0] = proj[:, 2 * na_w:3 * na_w].astype(BF16)
    o = 3 * na_w
    cqn = (_rms(proj[:, o:o + q_lora]) * qan_ref[...]).astype(BF16)
    o += q_lora
    ckvn = (_rms(proj[:, o:o + kv_lora]) * kvan_ref[...]).astype(BF16)
    o += kv_lora
    kra = proj[:, o:o + HEAD_PAD]
    krb = proj[:, o + HEAD_PAD:o + 2 * HEAD_PAD]
    qa = jnp.dot(cqn, wqa_ref[...], preferred_element_type=F32)
    qb = jnp.dot(cqn, wqb_ref[...], preferred_element_type=F32)
    kn = jnp.dot(ckvn, wk_ref[...], preferred_element_type=F32)
    vv = jnp.dot(ckvn, wv_ref[...], preferred_element_type=F32)
    ck = ck_ref[...]
    sk = sk_ref[...]
    lane = lax.broadcasted_iota(jnp.int32, (1, HEAD_PAD), 1)
    nope = (lane < MLA_NOPE).astype(F32)
    ones_col = (lane == MLA_V).astype(F32)
    cq = (nope + ck) * qscale
    sq = sk * qscale
    kpe = kra * ck + krb * sk
    for hd in range(MLA_HEADS):
        sl = slice(hd * HEAD_PAD, (hd + 1) * HEAD_PAD)
        q_ref[0, :, sl] = (qa[:, sl] * cq + qb[:, sl] * sq).astype(BF16)
        k_ref[0, :, sl] = (kn[:, sl] + kpe).astype(BF16)
        v_ref[0, :, sl] = (vv[:, sl] + ones_col).astype(BF16)


def _inproj(x, ada3, p, ck, sk):
    b, s, d = x.shape
    t = min(T_PROJ, s)
    na_w = NA_HEADS * HEAD_DIM
    hw = MLA_HEADS * HEAD_PAD
    q_lora = p['wqa'].shape[0]
    kv_lora = p['wk'].shape[0]
    w1 = p['w1']
    kern = functools.partial(_inproj_kernel, na_w=na_w, q_lora=q_lora, kv_lora=kv_lora,
                             qscale=LOG2E / float(np.sqrt(MLA_NOPE + MLA_ROPE)))
    full = lambda a: pl.BlockSpec(a.shape, lambda i, j: (0,) * a.ndim)
    tok = lambda w: pl.BlockSpec((1, t, w), lambda i, j: (i, j, 0))
    outs = pl.pallas_call(
        kern,
        grid=(b, s // t),
        in_specs=[tok(d),
                  pl.BlockSpec((1, 6, d), lambda i, j: (i, 0, 0)),
                  full(p['ln_pre_mix']),
                  pl.BlockSpec((t, HEAD_PAD), lambda i, j: (j, 0)),
                  pl.BlockSpec((t, HEAD_PAD), lambda i, j: (j, 0)),
                  full(w1), full(p['q_a_norm']), full(p['wqa']), full(p['wqb']),
                  full(p['kv_a_norm']), full(p['wk']), full(p['wv'])],
        out_specs=[tok(na_w), tok(na_w), tok(na_w), tok(hw), tok(hw), tok(hw)],
        out_shape=[jax.ShapeDtypeStruct((b, s, na_w), BF16)] * 3
                  + [jax.ShapeDtypeStruct((b, s, hw), BF16)] * 3,
        compiler_params=_cparams(("arbitrary", "arbitrary")),
        name="inproj",
    )(x, ada3, p['ln_pre_mix'], ck, sk, w1, p['q_a_norm'], p['wqa'], p['wqb'],
      p['kv_a_norm'], p['wk'], p['wv'])
    return outs


def _na_kernel(q_ref, kp_ref, kc_ref, kn_ref, vp_ref, vc_ref, vn_ref, bias_ref, o_ref,
               kcat, vcat, *, rows):
    blk = pl.program_id(1)
    nb = NA_ROWS * GRID_W
    win = NA_KR * GRID_W
    for i, (kr, vr) in enumerate(((kp_ref, vp_ref), (kc_ref, vc_ref), (kn_ref, vn_ref))):
        kcat[i * nb:(i + 1) * nb, :] = kr[0]
        vcat[i * nb:(i + 1) * nb, :] = vr[0]
    lane = lax.broadcasted_iota(jnp.int32, (GRID_W, LANE), 1)
    lo = lane < HEAD_DIM

    def row_body(rl, carry):
        r = blk * NA_ROWS + rl
        rs = jnp.clip(r - NA_KR // 2, 0, rows - NA_KR)
        d = r - rs
        off = pl.multiple_of((rs - (blk - 1) * NA_ROWS) * GRID_W, GRID_W)
        qoff = pl.multiple_of(rl * GRID_W, GRID_W)
        scores = []
        for hp in range(NA_HEADS // 2):
            sl = slice(hp * LANE, (hp + 1) * LANE)
            qp = q_ref[0, pl.ds(qoff, GRID_W), sl]
            kw = kcat[pl.ds(off, win), sl]
            zero = jnp.zeros_like(qp)
            qm = jnp.concatenate([jnp.where(lo, qp, zero), jnp.where(lo, zero, qp)], axis=0)
            s = lax.dot_general(qm, kw, (((1,), (1,)), ((), ())), preferred_element_type=F32)
            scores.append(s + bias_ref[d, hp])
        probs, inv_l = [], []
        for s in scores:
            m = jnp.max(s, axis=-1, keepdims=True)
            e = jnp.exp(s - m)
            inv_l.append(1.0 / jnp.sum(e, axis=-1, keepdims=True))
            probs.append(e.astype(BF16))
        pair_outs = []
        for hp in range(NA_HEADS // 2):
            vw = vcat[pl.ds(off, win), hp * LANE:(hp + 1) * LANE]
            o2 = jnp.dot(probs[hp], vw, preferred_element_type=F32) * inv_l[hp]
            pair_outs.append(jnp.where(lo, o2[:GRID_W], o2[GRID_W:]))
        o_ref[0, pl.ds(qoff, GRID_W), :] = jnp.concatenate(pair_outs, axis=1)
        return carry

    lax.fori_loop(0, NA_ROWS, row_body, 0, unroll=4)


def _na_bias_table(rpb):
    nh, nrow, nrel = rpb.shape
    cols = np.arange(GRID_W)
    cstart = np.clip(cols - NA_KC // 2, 0, GRID_W - NA_KC)
    j = np.arange(GRID_W)
    inwin = (j[None, :] >= cstart[:, None]) & (j[None, :] < cstart[:, None] + NA_KC)
    period = 2 * GRID_W - 1
    vpad = jnp.concatenate([rpb[..., NA_KC - 1:], jnp.zeros((nh, nrow, period - nrel), rpb.dtype),
                            rpb[..., :NA_KC - 1]], axis=-1)
    toep = jnp.tile(vpad, (1, 1, GRID_W))[..., :GRID_W * (period - 1)]
    toep = toep.reshape(nh, nrow, GRID_W, period - 1)[..., :GRID_W]
    tabs = []
    for d in range(NA_KR):
        t = toep[:, NA_KR - 1 - d:2 * NA_KR - 1 - d]
        tabs.append(t.transpose(0, 2, 1, 3))
    tab = jnp.stack(tabs, axis=0).astype(F32)
    tab = jnp.where(jnp.asarray(inwin)[None, None, :, None, :], tab, NEG_BIG)
    return tab.reshape(NA_KR, nh // 2, 2 * GRID_W, NA_KR * GRID_W)


def _na(q, k, v, bias):
    b, s, w = q.shape
    rows = s // GRID_W
    assert rows % NA_ROWS == 0 and rows >= 2 * NA_ROWS
    nblk = rows // NA_ROWS
    nb = NA_ROWS * GRID_W
    cur = pl.BlockSpec((1, nb, w), lambda i, j: (i, j, 0))
    prv = pl.BlockSpec((1, nb, w), lambda i, j: (i, jnp.maximum(j - 1, 0), 0))
    nxt = pl.BlockSpec((1, nb, w), lambda i, j: (i, jnp.minimum(j + 1, nblk - 1), 0))
    return pl.pallas_call(
        functools.partial(_na_kernel, rows=rows),
        grid=(b, nblk),
        in_specs=[cur, prv, cur, nxt, prv, cur, nxt,
                  pl.BlockSpec(bias.shape, lambda i, j: (0, 0, 0, 0))],
        out_specs=cur,
        out_shape=jax.ShapeDtypeStruct((b, s, w), F32),
        scratch_shapes=[pltpu.VMEM((3 * nb, w), BF16), pltpu.VMEM((3 * nb, w), BF16)],
        compiler_params=_cparams(("arbitrary", "arbitrary")),
        name="na",
    )(q, k, k, k, v, v, v, bias)


def _flash_kernel(q_ref, k_ref, v_ref, o_ref, m_sc, acc_sc, s_sc, *, tk, nk, unroll):
    q = q_ref[0]
    m_sc[...] = jnp.full(m_sc.shape, -jnp.inf, F32)
    acc_sc[...] = jnp.zeros(acc_sc.shape, F32)

    def scores(j):
        off = pl.multiple_of(j * tk, tk)
        ks = k_ref[0, pl.ds(off, tk), :]
        return lax.dot_general(q, ks, (((1,), (1,)), ((), ())), preferred_element_type=F32)

    def accumulate(s, j):
        off = pl.multiple_of(j * tk, tk)
        vs = v_ref[0, pl.ds(off, tk), :]
        cols = [s[:, c * LANE:(c + 1) * LANE] for c in range(tk // LANE)]
        cm = cols[0]
        for sc in cols[1:]:
            cm = jnp.maximum(cm, sc)
        m_prev = m_sc[...]
        m_new = jnp.maximum(m_prev, jnp.max(cm, axis=-1, keepdims=True))
        alpha = jnp.exp2(m_prev - m_new)
        p = jnp.concatenate([jnp.exp2((sc - m_new).astype(BF16)) for sc in cols], axis=1)
        acc_sc[...] = alpha * acc_sc[...] + jnp.dot(p, vs, preferred_element_type=F32)
        m_sc[...] = m_new

    s_sc[...] = scores(0)

    def trip(i, carry):
        s_cur = s_sc[...]
        for u in range(unroll):
            j = i * unroll + u
            s_next = scores(jnp.minimum(j + 1, nk - 1))
            accumulate(s_cur, j)
            s_cur = s_next
        s_sc[...] = s_cur
        return carry

    lax.fori_loop(0, nk // unroll, trip, 0)
    acc = acc_sc[...]
    o_ref[0] = acc / acc[:, MLA_V:MLA_V + 1]


def _flash(q, k, v):
    b, s, hw = q.shape
    nh = hw // HEAD_PAD
    tq = min(T_Q, s)
    tk = min(T_K, s)
    nk = s // tk
    unroll = 16 if nk % 16 == 0 else (2 if nk % 2 == 0 else 1)
    return pl.pallas_call(
        functools.partial(_flash_kernel, tk=tk, nk=nk, unroll=unroll),
        grid=(b, nh, s // tq),
        in_specs=[pl.BlockSpec((1, tq, HEAD_PAD), lambda i, h, j: (i, j, h)),
                  pl.BlockSpec((1, s, HEAD_PAD), lambda i, h, j: (i, 0, h)),
                  pl.BlockSpec((1, s, HEAD_PAD), lambda i, h, j: (i, 0, h))],
        out_specs=pl.BlockSpec((1, tq, HEAD_PAD), lambda i, h, j: (i, j, h)),
        out_shape=jax.ShapeDtypeStruct((b, s, hw), F32),
        scratch_shapes=[pltpu.VMEM((tq, HEAD_PAD), F32), pltpu.VMEM((tq, HEAD_PAD), F32),
                        pltpu.VMEM((tq, tk), F32)],
        compiler_params=_cparams(("arbitrary", "arbitrary", "arbitrary")),
        name="flash",
    )(q, k, v)


def _post_kernel(na_ref, mla_ref, x_ref, ada_ref, gna_ref, gmla_ref, wona_ref, womla_ref,
                 lnpost_ref, lnpre_ref, wr_ref, br_ref,
                 x1_ref, h2_ref, e_ref, g_ref, rank_ref, cnt_ref, base_sc):
    first = jnp.logical_and(pl.program_id(0) == 0, pl.program_id(1) == 0)

    @pl.when(first)
    def _():
        base_sc[...] = jnp.zeros(base_sc.shape, F32)

    t = x_ref.shape[1]
    na = na_ref[0]
    nan_ = _rms(na) * gna_ref[...]
    ml = mla_ref[0]
    lane_w = lax.broadcasted_iota(jnp.int32, ml.shape, 1)
    ml = jnp.where((lane_w % HEAD_PAD) < MLA_V, ml, 0.0)
    mln = _rms(ml, n=MLA_HEADS * MLA_V) * gmla_ref[...]
    mix = (jnp.dot(nan_.astype(BF16), wona_ref[...], preferred_element_type=F32)
           + jnp.dot(mln.astype(BF16), womla_ref[...], preferred_element_type=F32))
    g_a = ada_ref[0, 2:3, :]
    sh_m = ada_ref[0, 3:4, :]
    sc_m = ada_ref[0, 4:5, :]
    x1 = x_ref[0] + g_a * (_rms(mix) * lnpost_ref[...])
    x1_ref[0] = x1
    h2 = (_rms(x1) * lnpre_ref[...]) * (1.0 + sc_m) + sh_m
    _store_token_tiles(h2_ref.at[0], h2)

    h_hi = h2.astype(BF16)
    h_lo = (h2 - h_hi.astype(F32)).astype(BF16)
    logits = jnp.dot(jnp.concatenate([h_hi, h_lo, h_hi], axis=1), wr_ref[...],
                     preferred_element_type=F32) + br_ref[...]
    lane = lax.broadcasted_iota(jnp.int32, (t, LANE), 1)
    lane_f = lane.astype(F32)
    work = jnp.where(lane < N_EXPERTS, logits, -jnp.inf)
    vals, idxs = [], []
    onehot = jnp.zeros((t, LANE), F32)
    for _ in range(TOP_K):
        mk = jnp.max(work, axis=-1, keepdims=True)
        ik = jnp.min(jnp.where(work == mk, lane_f, float(LANE)), axis=-1, keepdims=True)
        sel = lane_f == ik
        work = jnp.where(sel, -jnp.inf, work)
        onehot = jnp.where(sel, 1.0, onehot)
        vals.append(mk)
        idxs.append(ik)
    exps = [jnp.exp(vk - vals[0]) for vk in vals]
    denom = exps[0]
    for ek in exps[1:]:
        denom = denom + ek

    r_i = lax.broadcasted_iota(jnp.int32, (t, t), 0)
    c_i = lax.broadcasted_iota(jnp.int32, (t, t), 1)
    tril = jnp.where(c_i < r_i, 1.0, 0.0).astype(BF16)
    pref = jnp.dot(tril, onehot.astype(BF16), preferred_element_type=F32) + base_sc[...]

    e_out = jnp.zeros((t, LANE), F32)
    g_out = jnp.zeros((t, LANE), F32)
    r_out = jnp.zeros((t, LANE), F32)
    for kk in range(TOP_K):
        rk = jnp.sum(jnp.where(lane_f == idxs[kk], pref, 0.0), axis=-1, keepdims=True)
        here = lane == kk
        e_out = jnp.where(here, idxs[kk], e_out)
        g_out = jnp.where(here, exps[kk] / denom, g_out)
        r_out = jnp.where(here, rk, r_out)
    e_ref[0] = e_out.astype(jnp.int32)
    g_ref[0] = g_out
    rank_ref[0] = r_out.astype(jnp.int32)
    base_sc[...] = base_sc[...] + jnp.sum(onehot, axis=0, keepdims=True)
    cnt_ref[...] = jnp.broadcast_to(base_sc[...], cnt_ref.shape)


def _post(na_out, mla_out, x, ada3, p):
    b, s, d = x.shape
    t = min(T_POST, s)
    full = lambda a: pl.BlockSpec(a.shape, lambda i, j: (0,) * a.ndim)
    tok = lambda w: pl.BlockSpec((1, t, w), lambda i, j: (i, j, 0))
    return pl.pallas_call(
        _post_kernel,
        grid=(b, s // t),
        in_specs=[tok(na_out.shape[-1]), tok(mla_out.shape[-1]), tok(d),
                  pl.BlockSpec((1, 6, d), lambda i, j: (i, 0, 0)),
                  full(p['g_na_out']), full(p['g_mla_pad']), full(p['wo_na']), full(p['wo_mla']),
                  full(p['ln_post_mix']), full(p['ln_pre_moe']), full(p['w_router']), full(p['b_router'])],
        out_specs=[tok(d), pl.BlockSpec((1, t * (d // LANE), LANE), lambda i, j: (i, j, 0)),
                   tok(LANE), tok(LANE), tok(LANE),
                   pl.BlockSpec((8, LANE), lambda i, j: (0, 0))],
        out_shape=[jax.ShapeDtypeStruct((b, s, d), F32), jax.ShapeDtypeStruct((b, s * (d // LANE), LANE), F32),
                   jax.ShapeDtypeStruct((b, s, LANE), jnp.int32), jax.ShapeDtypeStruct((b, s, LANE), F32),
                   jax.ShapeDtypeStruct((b, s, LANE), jnp.int32), jax.ShapeDtypeStruct((8, LANE), F32)],
        scratch_shapes=[pltpu.VMEM((1, LANE), F32)],
        compiler_params=_cparams(("arbitrary", "arbitrary")),
        name="post",
    )(na_out, mla_out, x, ada3, p['g_na_out'], p['g_mla_pad'], p['wo_na'], p['wo_mla'],
      p['ln_post_mix'], p['ln_pre_moe'], p['w_router'], p['b_router'])


def _experts_kernel(be_ref, nact_ref, tok_hbm, h_hbm, w1g_ref, w1l_ref, w2_ref, b1g_ref, b1l_ref, b2_ref,
                    y_ref, idx_sm, xbuf, isem, rsem, *, blk, nc):
    i = pl.program_id(0)
    nact = nact_ref[0]

    def idx_copy(step, slot):
        return pltpu.make_async_copy(tok_hbm.at[step], idx_sm.at[slot], isem.at[slot])

    def issue_rows(slot):
        def one(j, carry):
            src = pl.multiple_of(idx_sm[slot, j], nc)
            dst = pl.multiple_of(j * nc, nc)
            pltpu.make_async_copy(h_hbm.at[pl.ds(src, nc), :], xbuf.at[slot, pl.ds(dst, nc), :],
                                  rsem.at[slot]).start()
            return carry
        lax.fori_loop(0, blk, one, 0, unroll=16)

    def wait_rows(slot):
        pltpu.make_async_copy(h_hbm.at[pl.ds(0, blk * nc), :], xbuf.at[slot], rsem.at[slot]).wait()

    @pl.when(jnp.logical_and(i == 0, nact > 0))
    def _():
        idx_copy(0, 0).start()
        idx_copy(0, 0).wait()
        issue_rows(0)

        @pl.when(nact > 1)
        def _():
            idx_copy(1, 1).start()

    cur = lax.rem(i, 2)
    nxt = 1 - cur

    has_next = i + 1 < nact

    @pl.when(has_next)
    def _():
        idx_copy(i + 1, nxt).wait()

    @pl.when(i < nact)
    def _():
        wait_rows(cur)
        for j in range(blk):
            @pl.when(has_next)
            def _():
                src = pl.multiple_of(idx_sm[nxt, j], nc)
                pltpu.make_async_copy(h_hbm.at[pl.ds(src, nc), :], xbuf.at[nxt, pl.ds(j * nc, nc), :],
                                      rsem.at[nxt]).start()
        x = _load_token_tiles(xbuf.at[cur], nc).astype(BF16)
        ug = jnp.dot(x, w1g_ref[0], preferred_element_type=F32) + b1g_ref[0]
        ul = jnp.dot(x, w1l_ref[0], preferred_element_type=F32) + b1l_ref[0]
        x_glu = jnp.minimum(ug, SWIGLU_LIMIT)
        x_lin = jnp.clip(ul, -SWIGLU_LIMIT, SWIGLU_LIMIT)
        act = (x_lin + 1.0) * (x_glu * (1.0 / (1.0 + jnp.exp(-SWIGLU_ALPHA * x_glu))))
        y = jnp.dot(act.astype(BF16), w2_ref[0], preferred_element_type=F32) + b2_ref[0]
        _store_token_tiles(y_ref, y)

    @pl.when(i + 2 < nact)
    def _():
        idx_copy(i + 2, cur).start()

    @pl.when(i >= nact)
    def _():
        y_ref[...] = jnp.zeros(y_ref.shape, F32)


def _experts(h2, slot_tok, block_e, nact, p, nc):
    nb, blk = slot_tok.shape
    wspec = lambda a: pl.BlockSpec((1,) + a.shape[1:], lambda i, be, na: (be[i], 0, 0))
    grid_spec = pltpu.PrefetchScalarGridSpec(
        num_scalar_prefetch=2,
        grid=(nb,),
        in_specs=[pl.BlockSpec(memory_space=pl.ANY), pl.BlockSpec(memory_space=pl.ANY),
                  wspec(p['w1g']), wspec(p['w1l']), wspec(p['w2']),
                  wspec(p['b1g']), wspec(p['b1l']), wspec(p['b2'])],
        out_specs=pl.BlockSpec((blk * nc, LANE), lambda i, be, na: (i, 0)),
        scratch_shapes=[pltpu.SMEM((2, blk), jnp.int32), pltpu.VMEM((2, blk * nc, LANE), F32),
                        pltpu.SemaphoreType.DMA((2,)), pltpu.SemaphoreType.DMA((2,))],
    )
    return pl.pallas_call(
        functools.partial(_experts_kernel, blk=blk, nc=nc),
        grid_spec=grid_spec,
        out_shape=jax.ShapeDtypeStruct((nb * blk * nc, LANE), F32),
        compiler_params=_cparams(("arbitrary",)),
        name="experts",
    )(block_e, nact, slot_tok, h2, p['w1g'], p['w1l'], p['w2'], p['b1g'], p['b1l'], p['b2'])


def _deint_kernel(w_ref, sel_ref, g_ref, l_ref):
    sel = sel_ref[...]
    for blk in range(w_ref.shape[2] // (2 * LANE)):
        w = w_ref[0, :, blk * 2 * LANE:(blk + 1) * 2 * LANE].astype(BF16)
        r = jnp.dot(w, sel, preferred_element_type=F32).astype(BF16)
        g_ref[0, :, blk * LANE:(blk + 1) * LANE] = r[:, :LANE]
        l_ref[0, :, blk * LANE:(blk + 1) * LANE] = r[:, LANE:]


def _deinterleave_w1(w_mlp1):
    ne, d, f2 = w_mlp1.shape
    f = f2 // 2
    rows = min(512, d)
    src = lax.broadcasted_iota(jnp.int32, (2 * LANE, 2 * LANE), 0)
    dst = lax.broadcasted_iota(jnp.int32, (2 * LANE, 2 * LANE), 1)
    sel = jnp.where(dst < LANE, src == 2 * dst, src == 2 * (dst - LANE) + 1).astype(BF16)
    return pl.pallas_call(
        _deint_kernel,
        grid=(ne, d // rows),
        in_specs=[pl.BlockSpec((1, rows, f2), lambda e, r: (e, r, 0)),
                  pl.BlockSpec((2 * LANE, 2 * LANE), lambda e, r: (0, 0))],
        out_specs=[pl.BlockSpec((1, rows, f), lambda e, r: (e, r, 0))] * 2,
        out_shape=[jax.ShapeDtypeStruct((ne, d, f), BF16)] * 2,
        compiler_params=_cparams(("arbitrary", "arbitrary")),
        name="deint",
    )(w_mlp1, sel)


def _combine_kernel(dest_hbm, y_hbm, x1_ref, g_ref, ada_ref, ln_ref, o_ref,
                    idx_sm, ybuf, isem, rsem, *, t, nsteps, nc):
    i = pl.program_id(0) * pl.num_programs(1) + pl.program_id(1)

    def idx_copy(step, slot):
        return pltpu.make_async_copy(dest_hbm.at[step], idx_sm.at[slot], isem.at[slot])

    def issue_rows(slot):
        def one(j, carry):
            for kk in range(TOP_K):
                src = pl.multiple_of(idx_sm[slot, j * TOP_K + kk], nc)
                dst = pl.multiple_of(j * nc, nc)
                pltpu.make_async_copy(y_hbm.at[pl.ds(src, nc), :], ybuf.at[slot, kk, pl.ds(dst, nc), :],
                                      rsem.at[slot]).start()
            return carry
        lax.fori_loop(0, t, one, 0, unroll=4)

    def wait_rows(slot):
        for kk in range(TOP_K):
            pltpu.make_async_copy(y_hbm.at[pl.ds(0, t * nc), :], ybuf.at[slot, kk], rsem.at[slot]).wait()

    @pl.when(i == 0)
    def _():
        idx_copy(0, 0).start()
        idx_copy(0, 0).wait()
        issue_rows(0)
        if nsteps > 1:
            idx_copy(1, 1).start()

    cur = lax.rem(i, 2)
    nxt = 1 - cur

    has_next = i + 1 < nsteps

    @pl.when(has_next)
    def _():
        idx_copy(i + 1, nxt).wait()

    for j in range(t):
        for kk in range(TOP_K):
            @pl.when(has_next)
            def _():
                src = pl.multiple_of(idx_sm[nxt, j * TOP_K + kk], nc)
                pltpu.make_async_copy(y_hbm.at[pl.ds(src, nc), :], ybuf.at[nxt, kk, pl.ds(j * nc, nc), :],
                                      rsem.at[nxt]).start()

    @pl.when(i + 2 < nsteps)
    def _():
        idx_copy(i + 2, cur).start()

    wait_rows(cur)
    g = g_ref[0]
    y = jnp.zeros((t, x1_ref.shape[2]), F32)
    for kk in range(TOP_K):
        y = y + _load_token_tiles(ybuf.at[cur, kk], nc) * g[:, kk:kk + 1]
    g_m = ada_ref[0, 5:6, :]
    o_ref[0] = x1_ref[0] + g_m * (_rms(y) * ln_ref[...])


def _combine(dest, y_sorted, x1, gates, ada3, ln_post_moe):
    b, s, d = x1.shape
    t = min(T_COMB, s)
    ns = s // t
    nsteps = b * ns
    nc = d // LANE
    dest2 = (dest * nc).reshape(nsteps, t * TOP_K)
    return pl.pallas_call(
        functools.partial(_combine_kernel, t=t, nsteps=nsteps, nc=nc),
        grid=(b, ns),
        in_specs=[pl.BlockSpec(memory_space=pl.ANY), pl.BlockSpec(memory_space=pl.ANY),
                  pl.BlockSpec((1, t, d), lambda i, j: (i, j, 0)),
                  pl.BlockSpec((1, t, LANE), lambda i, j: (i, j, 0)),
                  pl.BlockSpec((1, 6, d), lambda i, j: (i, 0, 0)),
                  pl.BlockSpec(ln_post_moe.shape, lambda i, j: (0, 0))],
        out_specs=pl.BlockSpec((1, t, d), lambda i, j: (i, j, 0)),
        out_shape=jax.ShapeDtypeStruct((b, s, d), F32),
        scratch_shapes=[pltpu.SMEM((2, t * TOP_K), jnp.int32),
                        pltpu.VMEM((2, TOP_K, t * nc, LANE), F32),
                        pltpu.SemaphoreType.DMA((2,)), pltpu.SemaphoreType.DMA((2,))],
        compiler_params=_cparams(("arbitrary", "arbitrary")),
        name="combine",
    )(dest2, y_sorted, x1, gates, ada3, ln_post_moe)


def _rope_partner(w):
    half = MLA_ROPE // 2
    return jnp.concatenate([-w[..., half:], w[..., :half]], axis=-1)


def _prep_params(ln_pre_mix, ln_post_mix, ln_pre_moe, ln_post_moe, w_in, na_rpb, q_a_norm, w_q_b,
                 kv_a_norm, w_kv_b, g_na_out, g_mla_out, w_o, w_router, b_router,
                 w_mlp1, b_mlp1, w_mlp2, b_mlp2):
    d = w_in.shape[0]
    na_w = NA_HEADS * HEAD_DIM
    q_lora = w_q_b.shape[0]
    kv_lora = w_kv_b.shape[0]
    qk = MLA_NOPE + MLA_ROPE
    main_w = 3 * na_w + q_lora + kv_lora
    w_kr = w_in[:, main_w:main_w + MLA_ROPE]
    zpad = lambda n: jnp.zeros((d, n), F32)
    kra = jnp.concatenate([zpad(MLA_NOPE), w_kr, zpad(HEAD_PAD - qk)], axis=1)
    krb = jnp.concatenate([zpad(MLA_NOPE), _rope_partner(w_kr), zpad(HEAD_PAD - qk)], axis=1)
    w1 = jnp.concatenate([w_in[:, :main_w], kra, krb], axis=1).astype(BF16)

    wq = w_q_b.reshape(q_lora, MLA_HEADS, qk)
    zq = lambda n: jnp.zeros((q_lora, MLA_HEADS, n), F32)
    wqa = jnp.concatenate([wq, zq(HEAD_PAD - qk)], axis=2).reshape(q_lora, -1).astype(BF16)
    wqb = jnp.concatenate([zq(MLA_NOPE), _rope_partner(wq[:, :, MLA_NOPE:]), zq(HEAD_PAD - qk)],
                          axis=2).reshape(q_lora, -1).astype(BF16)
    wkv = w_kv_b.reshape(kv_lora, MLA_HEADS, MLA_NOPE + MLA_V)
    zk = lambda n: jnp.zeros((kv_lora, MLA_HEADS, n), F32)
    wk = jnp.concatenate([wkv[:, :, :MLA_NOPE], zk(HEAD_PAD - MLA_NOPE)], axis=2).reshape(kv_lora, -1).astype(BF16)
    wv = jnp.concatenate([wkv[:, :, MLA_NOPE:], zk(HEAD_PAD - MLA_V)], axis=2).reshape(kv_lora, -1).astype(BF16)

    mla_w = MLA_HEADS * MLA_V
    g_mla_pad = jnp.concatenate([g_mla_out.reshape(MLA_HEADS, MLA_V),
                                 jnp.zeros((MLA_HEADS, HEAD_PAD - MLA_V), F32)], axis=1).reshape(1, -1)
    wo_mla = w_o[na_w:na_w + mla_w].reshape(MLA_HEADS, MLA_V, d)
    wo_mla = jnp.concatenate([wo_mla, jnp.zeros((MLA_HEADS, HEAD_PAD - MLA_V, d), F32)],
                             axis=1).reshape(MLA_HEADS * HEAD_PAD, d).astype(BF16)
    w_r = jnp.concatenate([w_router, jnp.zeros((d, LANE - N_EXPERTS), F32)], axis=1)
    w_r_hi = w_r.astype(BF16)
    w_r_lo = (w_r - w_r_hi.astype(F32)).astype(BF16)
    w_r = jnp.concatenate([w_r_hi, w_r_hi, w_r_lo], axis=0)
    b_r = jnp.concatenate([b_router, jnp.zeros((LANE - N_EXPERTS,), F32)]).reshape(1, LANE)
    ne = w_mlp1.shape[0]
    w1g, w1l = _deinterleave_w1(w_mlp1)
    return {
        'ln_pre_mix': ln_pre_mix.reshape(1, d), 'ln_post_mix': ln_post_mix.reshape(1, d),
        'ln_pre_moe': ln_pre_moe.reshape(1, d), 'ln_post_moe': ln_post_moe.reshape(1, d),
        'w1': w1, 'q_a_norm': q_a_norm.reshape(1, q_lora), 'wqa': wqa, 'wqb': wqb,
        'kv_a_norm': kv_a_norm.reshape(1, kv_lora), 'wk': wk, 'wv': wv,
        'na_bias': _na_bias_table(na_rpb),
        'g_na_out': g_na_out.reshape(1, na_w), 'g_mla_pad': g_mla_pad,
        'wo_na': w_o[:na_w].astype(BF16), 'wo_mla': wo_mla,
        'w_router': w_r, 'b_router': b_r,
        'w1g': w1g, 'w1l': w1l,
        'w2': w_mlp2.astype(BF16),
        'b1g': b_mlp1[:, 0::2].reshape(ne, 1, -1), 'b1l': b_mlp1[:, 1::2].reshape(ne, 1, -1),
        'b2': b_mlp2.reshape(ne, 1, -1),
    }


def _rope_lane_tables(s):
    half = MLA_ROPE // 2
    inv = ROPE_THETA ** (-jnp.arange(half, dtype=F32) / half)
    ang = jnp.arange(s, dtype=F32)[:, None] * inv[None, :]
    cos, sin = jnp.cos(ang), jnp.sin(ang)
    z = lambda n: jnp.zeros((s, n), F32)
    tail = HEAD_PAD - MLA_NOPE - MLA_ROPE
    ck = jnp.concatenate([z(MLA_NOPE), cos, cos, z(tail)], axis=1)
    sk = jnp.concatenate([z(MLA_NOPE), sin, sin, z(tail)], axis=1)
    return ck, sk


def _moe_plan(e4, rank4, counts, n, nc):
    nk = n * TOP_K
    padded = (counts + MOE_BLK - 1) // MOE_BLK * MOE_BLK
    pad_end = jnp.cumsum(padded)
    pad_start = pad_end - padded
    dest = pad_start[e4] + rank4
    nb = (nk + N_EXPERTS * (MOE_BLK - 1) + MOE_BLK - 1) // MOE_BLK
    tok = jnp.broadcast_to(jnp.arange(n, dtype=jnp.int32)[:, None] * nc, (n, TOP_K))
    slot_tok = jnp.zeros((nb * MOE_BLK,), jnp.int32).at[dest.reshape(-1)].set(
        tok.reshape(-1), unique_indices=True, mode='promise_in_bounds')
    block_start = jnp.arange(nb, dtype=jnp.int32) * MOE_BLK
    block_e = jnp.minimum(jnp.sum((pad_end[None, :] <= block_start[:, None]).astype(jnp.int32), axis=1),
                          N_EXPERTS - 1)
    nact = (pad_end[-1] // MOE_BLK).astype(jnp.int32).reshape(1)
    return dest.astype(jnp.int32), slot_tok.reshape(nb, MOE_BLK), block_e, nact


def _encoder_layer(x, ada3, p):
    b, s, d = x.shape
    ck, sk = _rope_lane_tables(s)
    qna, kna, vna, q, k, v = _inproj(x, ada3, p, ck, sk)
    na_out = _na(qna, kna, vna, p['na_bias'])
    mla_out = _flash(q, k, v)
    x1, h2, e, g, rank, cnt = _post(na_out, mla_out, x, ada3, p)
    n = b * s
    e4 = e.reshape(n, LANE)[:, :TOP_K]
    rank4 = rank.reshape(n, LANE)[:, :TOP_K]
    counts = cnt[0, :N_EXPERTS].astype(jnp.int32)
    nc = d // LANE
    dest, slot_tok, block_e, nact = _moe_plan(e4, rank4, counts, n, nc)
    y_sorted = _experts(h2.reshape(n * nc, LANE), slot_tok, block_e, nact, p, nc)
    return _combine(dest, y_sorted, x1, g, ada3, p['ln_post_moe'])


def kernel(x_prompt, x_sample, c_prompt, c_sample, ln_pre_mix, ln_post_mix, ln_pre_moe, ln_post_moe, w_ada, b_ada, w_in, na_rpb, q_a_norm, w_q_b, kv_a_norm, w_kv_b, g_na_out, g_mla_out, w_o, w_router, b_router, w_mlp1, b_mlp1, w_mlp2, b_mlp2):
    depth = w_ada.shape[0]
    d = x_prompt.shape[-1]
    bp, bs = x_prompt.shape[0], x_sample.shape[0]
    rows = -(-(bp + bs) // 8) * 8
    y_prompt, y_sample = x_prompt, x_sample
    for l in range(depth):
        p = _prep_params(ln_pre_mix[l], ln_post_mix[l], ln_pre_moe[l], ln_post_moe[l], w_in[l], na_rpb[l],
                         q_a_norm[l], w_q_b[l], kv_a_norm[l], w_kv_b[l], g_na_out[l], g_mla_out[l], w_o[l],
                         w_router[l], b_router[l], w_mlp1[l], b_mlp1[l], w_mlp2[l], b_mlp2[l])
        c_all = jnp.concatenate([c_prompt, c_sample, jnp.zeros((rows - bp - bs, d), F32)], axis=0)
        ada3 = _ada(c_all, w_ada[l], b_ada[l]).reshape(rows, 6, d)
        y_prompt = _encoder_layer(y_prompt, ada3[:bp], p)
        y_sample = _encoder_layer(y_sample, ada3[bp:bp + bs], p)
    return (y_prompt, y_sample)
```

```python
import functools

import numpy as np
import jax
import jax.numpy as jnp
from jax import lax
from jax.experimental import pallas as pl
from jax.experimental.pallas import tpu as pltpu

F32 = jnp.float32
BF16 = jnp.bfloat16

GRID_W = 64
HEAD_DIM = 64
NA_HEADS = 8
NA_KR = 8
NA_KC = 16
MLA_HEADS = 8
MLA_NOPE = 64
MLA_ROPE = 32
MLA_V = 64
ROPE_THETA = 10000.0
N_EXPERTS = 32
TOP_K = 4
SWIGLU_LIMIT = 7.0
SWIGLU_ALPHA = 1.702
EPS = 1e-6

LANE = 128
HEAD_PAD = 128
NEG_BIG = -1e30
LOG2E = 1.4426950408889634

T_PROJ = 256
T_Q = 512
T_K = 512
NA_ROWS = 8
T_POST = 256
MOE_BLK = 512
T_COMB = 128
VMEM_LIMIT = 56 * 1024 * 1024


def _rms(x, n=None):
    n = x.shape[-1] if n is None else n
    return x * lax.rsqrt(jnp.sum(x * x, axis=-1, keepdims=True) * (1.0 / n) + EPS)


def _cparams(sem):
    return pltpu.CompilerParams(dimension_semantics=sem, vmem_limit_bytes=VMEM_LIMIT)


def _store_token_tiles(ref, x):
    rows, d = x.shape
    nc = d // LANE
    for c in range(nc):
        ref[pl.ds(c, rows, stride=nc), :] = x[:, c * LANE:(c + 1) * LANE]


def _load_token_tiles(ref, nc):
    rows = ref.shape[0] // nc
    return jnp.concatenate([ref[pl.ds(c, rows, stride=nc), :] for c in range(nc)], axis=1)


def _ada_kernel(c_ref, w_ref, b_ref, o_ref):
    c = c_ref[...]
    s = c / (1.0 + jnp.exp(-c))
    o_ref[...] = jnp.dot(s, w_ref[...], preferred_element_type=F32,
                         precision=lax.Precision.HIGHEST) + b_ref[...]


def _ada(c_all, w_ada, b_ada):
    rows, d = c_all.shape
    n_out = w_ada.shape[1]
    return pl.pallas_call(
        _ada_kernel,
        grid=(n_out // d,),
        in_specs=[pl.BlockSpec((rows, d), lambda j: (0, 0)),
                  pl.BlockSpec((d, d), lambda j: (0, j)),
                  pl.BlockSpec((1, d), lambda j: (0, j))],
        out_specs=pl.BlockSpec((rows, d), lambda j: (0, j)),
        out_shape=jax.ShapeDtypeStruct((rows, n_out), F32),
        compiler_params=_cparams(("arbitrary",)),
        name="ada",
    )(c_all, w_ada, b_ada.reshape(1, n_out))


def _inproj_kernel(x_ref, ada_ref, g_ref, ck_ref, sk_ref, w1_ref, qan_ref, wqa_ref, wqb_ref,
                   kvan_ref, wk_ref, wv_ref,
                   qna_ref, kna_ref, vna_ref, q_ref, k_ref, v_ref, *, na_w, q_lora, kv_lora, qscale):
    x = x_ref[0]
    sh = ada_ref[0, 0:1, :]
    sc = ada_ref[0, 1:2, :]
    h = (_rms(x) * g_ref[...]) * (1.0 + sc) + sh
    proj = jnp.dot(h.astype(BF16), w1_ref[...], preferred_element_type=F32)
    qna_ref[0] = (proj[:, 0:na_w] * (1.0 / float(np.sqrt(HEAD_DIM)))).astype(BF16)
    kna_ref[0] = proj[:, na_w:2 * na_w].astype(BF16)
    vna_ref[0] = proj[:, 2 * na_w:3 * na_w].astype(BF16)
    o = 3 * na_w
    cqn = (_rms(proj[:, o:o + q_lora]) * qan_ref[...]).astype(BF16)
    o += q_lora
    ckvn = (_rms(proj[:, o:o + kv_lora]) * kvan_ref[...]).astype(BF16)
    o += kv_lora
    kra = proj[:, o:o + HEAD_PAD]
    krb = proj[:, o + HEAD_PAD:o + 2 * HEAD_PAD]
    qa = jnp.dot(cqn, wqa_ref[...], preferred_element_type=F32)
    qb = jnp.dot(cqn, wqb_ref[...], preferred_element_type=F32)
    kn = jnp.dot(ckvn, wk_ref[...], preferred_element_type=F32)
    vv = jnp.dot(ckvn, wv_ref[...], preferred_element_type=F32)
    ck = ck_ref[...]
    sk = sk_ref[...]
    lane = lax.broadcasted_iota(jnp.int32, (1, HEAD_PAD), 1)
    nope = (lane < MLA_NOPE).astype(F32)
    ones_col = (lane == MLA_V).astype(F32)
    cq = (nope + ck) * qscale
    sq = sk * qscale
    kpe = kra * ck + krb * sk
    for hd in range(MLA_HEADS):
        sl = slice(hd * HEAD_PAD, (hd + 1) * HEAD_PAD)
        q_ref[0, :, sl] = (qa[:, sl] * cq + qb[:, sl] * sq).astype(BF16)
        k_ref[0, :, sl] = (kn[:, sl] + kpe).astype(BF16)
        v_ref[0, :, sl] = (vv[:, sl] + ones_col).astype(BF16)


def _inproj(x, ada3, p, ck, sk):
    b, s, d = x.shape
    t = min(T_PROJ, s)
    na_w = NA_HEADS * HEAD_DIM
    hw = MLA_HEADS * HEAD_PAD
    q_lora = p['wqa'].shape[0]
    kv_lora = p['wk'].shape[0]
    w1 = p['w1']
    kern = functools.partial(_inproj_kernel, na_w=na_w, q_lora=q_lora, kv_lora=kv_lora,
                             qscale=LOG2E / float(np.sqrt(MLA_NOPE + MLA_ROPE)))
    full = lambda a: pl.BlockSpec(a.shape, lambda i, j: (0,) * a.ndim)
    tok = lambda w: pl.BlockSpec((1, t, w), lambda i, j: (i, j, 0))
    outs = pl.pallas_call(
        kern,
        grid=(b, s // t),
        in_specs=[tok(d),
                  pl.BlockSpec((1, 6, d), lambda i, j: (i, 0, 0)),
                  full(p['ln_pre_mix']),
                  pl.BlockSpec((t, HEAD_PAD), lambda i, j: (j, 0)),
                  pl.BlockSpec((t, HEAD_PAD), lambda i, j: (j, 0)),
                  full(w1), full(p['q_a_norm']), full(p['wqa']), full(p['wqb']),
                  full(p['kv_a_norm']), full(p['wk']), full(p['wv'])],
        out_specs=[tok(na_w), tok(na_w), tok(na_w), tok(hw), tok(hw), tok(hw)],
        out_shape=[jax.ShapeDtypeStruct((b, s, na_w), BF16)] * 3
                  + [jax.ShapeDtypeStruct((b, s, hw), BF16)] * 3,
        compiler_params=_cparams(("arbitrary", "arbitrary")),
        name="inproj",
    )(x, ada3, p['ln_pre_mix'], ck, sk, w1, p['q_a_norm'], p['wqa'], p['wqb'],
      p['kv_a_norm'], p['wk'], p['wv'])
    return outs


def _na_kernel(q_ref, kp_ref, kc_ref, kn_ref, vp_ref, vc_ref, vn_ref, bias_ref, o_ref,
               kcat, vcat, *, rows):
    blk = pl.program_id(1)
    nb = NA_ROWS * GRID_W
    win = NA_KR * GRID_W
    for i, (kr, vr) in enumerate(((kp_ref, vp_ref), (kc_ref, vc_ref), (kn_ref, vn_ref))):
        kcat[i * nb:(i + 1) * nb, :] = kr[0]
        vcat[i * nb:(i + 1) * nb, :] = vr[0]
    lane = lax.broadcasted_iota(jnp.int32, (GRID_W, LANE), 1)
    lo = lane < HEAD_DIM

    def row_body(rl, carry):
        r = blk * NA_ROWS + rl
        rs = jnp.clip(r - NA_KR // 2, 0, rows - NA_KR)
        d = r - rs
        off = pl.multiple_of((rs - (blk - 1) * NA_ROWS) * GRID_W, GRID_W)
        qoff = pl.multiple_of(rl * GRID_W, GRID_W)
        scores = []
        for hp in range(NA_HEADS // 2):
            sl = slice(hp * LANE, (hp + 1) * LANE)
            qp = q_ref[0, pl.ds(qoff, GRID_W), sl]
            kw = kcat[pl.ds(off, win), sl]
            zero = jnp.zeros_like(qp)
            qm = jnp.concatenate([jnp.where(lo, qp, zero), jnp.where(lo, zero, qp)], axis=0)
            s = lax.dot_general(qm, kw, (((1,), (1,)), ((), ())), preferred_element_type=F32)
            scores.append(s + bias_ref[d, hp])
        probs, inv_l = [], []
        for s in scores:
            m = jnp.max(s, axis=-1, keepdims=True)
            e = jnp.exp(s - m)
            inv_l.append(1.0 / jnp.sum(e, axis=-1, keepdims=True))
            probs.append(e.astype(BF16))
        pair_outs = []
        for hp in range(NA_HEADS // 2):
            vw = vcat[pl.ds(off, win), hp * LANE:(hp + 1) * LANE]
            o2 = jnp.dot(probs[hp], vw, preferred_element_type=F32) * inv_l[hp]
            pair_outs.append(jnp.where(lo, o2[:GRID_W], o2[GRID_W:]))
        o_ref[0, pl.ds(qoff, GRID_W), :] = jnp.concatenate(pair_outs, axis=1)
        return carry

    lax.fori_loop(0, NA_ROWS, row_body, 0, unroll=True)


def _na_bias_table(rpb):
    nh, nrow, nrel = rpb.shape
    cols = np.arange(GRID_W)
    cstart = np.clip(cols - NA_KC // 2, 0, GRID_W - NA_KC)
    j = np.arange(GRID_W)
    inwin = (j[None, :] >= cstart[:, None]) & (j[None, :] < cstart[:, None] + NA_KC)
    period = 2 * GRID_W - 1
    vpad = jnp.concatenate([rpb[..., NA_KC - 1:], jnp.zeros((nh, nrow, period - nrel), rpb.dtype),
                            rpb[..., :NA_KC - 1]], axis=-1)
    toep = jnp.tile(vpad, (1, 1, GRID_W))[..., :GRID_W * (period - 1)]
    toep = toep.reshape(nh, nrow, GRID_W, period - 1)[..., :GRID_W]
    tabs = []
    for d in range(NA_KR):
        t = toep[:, NA_KR - 1 - d:2 * NA_KR - 1 - d]
        tabs.append(t.transpose(0, 2, 1, 3))
    tab = jnp.stack(tabs, axis=0).astype(F32)
    tab = jnp.where(jnp.asarray(inwin)[None, None, :, None, :], tab, NEG_BIG)
    return tab.reshape(NA_KR, nh // 2, 2 * GRID_W, NA_KR * GRID_W)


def _na(q, k, v, bias):
    b, s, w = q.shape
    rows = s // GRID_W
    assert rows % NA_ROWS == 0 and rows >= 2 * NA_ROWS
    nblk = rows // NA_ROWS
    nb = NA_ROWS * GRID_W
    cur = pl.BlockSpec((1, nb, w), lambda i, j: (i, j, 0))
    prv = pl.BlockSpec((1, nb, w), lambda i, j: (i, jnp.maximum(j - 1, 0), 0))
    nxt = pl.BlockSpec((1, nb, w), lambda i, j: (i, jnp.minimum(j + 1, nblk - 1), 0))
    return pl.pallas_call(
        functools.partial(_na_kernel, rows=rows),
        grid=(b, nblk),
        in_specs=[cur, prv, cur, nxt, prv, cur, nxt,
                  pl.BlockSpec(bias.shape, lambda i, j: (0, 0, 0, 0))],
        out_specs=cur,
        out_shape=jax.ShapeDtypeStruct((b, s, w), F32),
        scratch_shapes=[pltpu.VMEM((3 * nb, w), BF16), pltpu.VMEM((3 * nb, w), BF16)],
        compiler_params=_cparams(("arbitrary", "arbitrary")),
        name="na",
    )(q, k, k, k, v, v, v, bias)


def _flash_kernel(q_ref, k_ref, v_ref, o_ref, m_sc, acc_sc, s_sc, *, tk, nk, unroll):
    q = q_ref[0]
    m_sc[...] = jnp.full(m_sc.shape, -jnp.inf, F32)
    acc_sc[...] = jnp.zeros(acc_sc.shape, F32)

    def scores(j):
        off = pl.multiple_of(j * tk, tk)
        ks = k_ref[0, pl.ds(off, tk), :]
        return lax.dot_general(q, ks, (((1,), (1,)), ((), ())), preferred_element_type=F32)

    def accumulate(s, j):
        off = pl.multiple_of(j * tk, tk)
        vs = v_ref[0, pl.ds(off, tk), :]
        cols = [s[:, c * LANE:(c + 1) * LANE] for c in range(tk // LANE)]
        cm = cols[0]
        for sc in cols[1:]:
            cm = jnp.maximum(cm, sc)
        m_prev = m_sc[...]
        m_new = jnp.maximum(m_prev, jnp.max(cm, axis=-1, keepdims=True))
        alpha = jnp.exp2(m_prev - m_new)
        p = jnp.concatenate([jnp.exp2((sc - m_new).astype(BF16)) for sc in cols], axis=1)
        acc_sc[...] = alpha * acc_sc[...] + jnp.dot(p, vs, preferred_element_type=F32)
        m_sc[...] = m_new

    s_sc[...] = scores(0)

    def trip(i, carry):
        s_cur = s_sc[...]
        for u in range(unroll):
            j = i * unroll + u
            s_next = scores(jnp.minimum(j + 1, nk - 1))
            accumulate(s_cur, j)
            s_cur = s_next
        s_sc[...] = s_cur
        return carry

    lax.fori_loop(0, nk // unroll, trip, 0)
    acc = acc_sc[...]
    o_ref[0] = acc / acc[:, MLA_V:MLA_V + 1]


def _flash(q, k, v):
    b, s, hw = q.shape
    nh = hw // HEAD_PAD
    tq = min(T_Q, s)
    tk = min(T_K, s)
    nk = s // tk
    unroll = next(u for u in (32, 16, 8, 4, 2, 1) if nk % u == 0)
    return pl.pallas_call(
        functools.partial(_flash_kernel, tk=tk, nk=nk, unroll=unroll),
        grid=(b, nh, s // tq),
        in_specs=[pl.BlockSpec((1, tq, HEAD_PAD), lambda i, h, j: (i, j, h)),
                  pl.BlockSpec((1, s, HEAD_PAD), lambda i, h, j: (i, 0, h)),
                  pl.BlockSpec((1, s, HEAD_PAD), lambda i, h, j: (i, 0, h))],
        out_specs=pl.BlockSpec((1, tq, HEAD_PAD), lambda i, h, j: (i, j, h)),
        out_shape=jax.ShapeDtypeStruct((b, s, hw), F32),
        scratch_shapes=[pltpu.VMEM((tq, HEAD_PAD), F32), pltpu.VMEM((tq, HEAD_PAD), F32),
                        pltpu.VMEM((tq, tk), F32)],
        compiler_params=_cparams(("arbitrary", "arbitrary", "arbitrary")),
        name="flash",
    )(q, k, v)


def _post_kernel(na_ref, mla_ref, x_ref, ada_ref, gna_ref, gmla_ref, wona_ref, womla_ref,
                 lnpost_ref, lnpre_ref, wr_ref, br_ref,
                 x1_ref, h2_ref, e_ref, g_ref, rank_ref, cnt_ref, base_sc):
    first = jnp.logical_and(pl.program_id(0) == 0, pl.program_id(1) == 0)

    @pl.when(first)
    def _():
        base_sc[...] = jnp.zeros(base_sc.shape, F32)

    t = x_ref.shape[1]
    na = na_ref[0]
    nan_ = _rms(na) * gna_ref[...]
    ml = mla_ref[0]
    lane_w = lax.broadcasted_iota(jnp.int32, ml.shape, 1)
    ml = jnp.where((lane_w % HEAD_PAD) < MLA_V, ml, 0.0)
    mln = _rms(ml, n=MLA_HEADS * MLA_V) * gmla_ref[...]
    mix = (jnp.dot(nan_.astype(BF16), wona_ref[...], preferred_element_type=F32)
           + jnp.dot(mln.astype(BF16), womla_ref[...], preferred_element_type=F32))
    g_a = ada_ref[0, 2:3, :]
    sh_m = ada_ref[0, 3:4, :]
    sc_m = ada_ref[0, 4:5, :]
    x1 = x_ref[0] + g_a * (_rms(mix) * lnpost_ref[...])
    x1_ref[0] = x1
    h2 = (_rms(x1) * lnpre_ref[...]) * (1.0 + sc_m) + sh_m
    _store_token_tiles(h2_ref.at[0], h2)

    h_hi = h2.astype(BF16)
    h_lo = (h2 - h_hi.astype(F32)).astype(BF16)
    logits = jnp.dot(jnp.concatenate([h_hi, h_lo, h_hi], axis=1), wr_ref[...],
                     preferred_element_type=F32) + br_ref[...]
    lane = lax.broadcasted_iota(jnp.int32, (t, LANE), 1)
    lane_f = lane.astype(F32)
    work = jnp.where(lane < N_EXPERTS, logits, -jnp.inf)
    vals, idxs = [], []
    onehot = jnp.zeros((t, LANE), F32)
    for _ in range(TOP_K):
        mk = jnp.max(work, axis=-1, keepdims=True)
        ik = jnp.min(jnp.where(work == mk, lane_f, float(LANE)), axis=-1, keepdims=True)
        sel = lane_f == ik
        work = jnp.where(sel, -jnp.inf, work)
        onehot = jnp.where(sel, 1.0, onehot)
        vals.append(mk)
        idxs.append(ik)
    exps = [jnp.exp(vk - vals[0]) for vk in vals]
    denom = exps[0]
    for ek in exps[1:]:
        denom = denom + ek

    r_i = lax.broadcasted_iota(jnp.int32, (t, t), 0)
    c_i = lax.broadcasted_iota(jnp.int32, (t, t), 1)
    tril = jnp.where(c_i < r_i, 1.0, 0.0).astype(BF16)
    pref = jnp.dot(tril, onehot.astype(BF16), preferred_element_type=F32) + base_sc[...]

    e_out = jnp.zeros((t, LANE), F32)
    g_out = jnp.zeros((t, LANE), F32)
    r_out = jnp.zeros((t, LANE), F32)
    for kk in range(TOP_K):
        rk = jnp.sum(jnp.where(lane_f == idxs[kk], pref, 0.0), axis=-1, keepdims=True)
        here = lane == kk
        e_out = jnp.where(here, idxs[kk], e_out)
        g_out = jnp.where(here, exps[kk] / denom, g_out)
        r_out = jnp.where(here, rk, r_out)
    e_ref[0] = e_out.astype(jnp.int32)
    g_ref[0] = g_out
    rank_ref[0] = r_out.astype(jnp.int32)
    base_sc[...] = base_sc[...] + jnp.sum(onehot, axis=0, keepdims=True)
    cnt_ref[...] = jnp.broadcast_to(base_sc[...], cnt_ref.shape)


def _post(na_out, mla_out, x, ada3, p):
    b, s, d = x.shape
    t = min(T_POST, s)
    full = lambda a: pl.BlockSpec(a.shape, lambda i, j: (0,) * a.ndim)
    tok = lambda w: pl.BlockSpec((1, t, w), lambda i, j: (i, j, 0))
    return pl.pallas_call(
        _post_kernel,
        grid=(b, s // t),
        in_specs=[tok(na_out.shape[-1]), tok(mla_out.shape[-1]), tok(d),
                  pl.BlockSpec((1, 6, d), lambda i, j: (i, 0, 0)),
                  full(p['g_na_out']), full(p['g_mla_pad']), full(p['wo_na']), full(p['wo_mla']),
                  full(p['ln_post_mix']), full(p['ln_pre_moe']), full(p['w_router']), full(p['b_router'])],
        out_specs=[tok(d), pl.BlockSpec((1, t * (d // LANE), LANE), lambda i, j: (i, j, 0)),
                   tok(LANE), tok(LANE), tok(LANE),
                   pl.BlockSpec((8, LANE), lambda i, j: (0, 0))],
        out_shape=[jax.ShapeDtypeStruct((b, s, d), F32), jax.ShapeDtypeStruct((b, s * (d // LANE), LANE), F32),
                   jax.ShapeDtypeStruct((b, s, LANE), jnp.int32), jax.ShapeDtypeStruct((b, s, LANE), F32),
                   jax.ShapeDtypeStruct((b, s, LANE), jnp.int32), jax.ShapeDtypeStruct((8, LANE), F32)],
        scratch_shapes=[pltpu.VMEM((1, LANE), F32)],
        compiler_params=_cparams(("arbitrary", "arbitrary")),
        name="post",
    )(na_out, mla_out, x, ada3, p['g_na_out'], p['g_mla_pad'], p['wo_na'], p['wo_mla'],
      p['ln_post_mix'], p['ln_pre_moe'], p['w_router'], p['b_router'])


def _experts_kernel(be_ref, nact_ref, tok_hbm, h_hbm, w1g_ref, w1l_ref, w2_ref, b1g_ref, b1l_ref, b2_ref,
                    y_ref, idx_sm, xbuf, isem, rsem, *, blk, nc):
    i = pl.program_id(0)
    nact = nact_ref[0]

    def idx_copy(step, slot):
        return pltpu.make_async_copy(tok_hbm.at[step], idx_sm.at[slot], isem.at[slot])

    def issue_rows(slot):
        def one(j, carry):
            src = pl.multiple_of(idx_sm[slot, j], nc)
            dst = pl.multiple_of(j * nc, nc)
            pltpu.make_async_copy(h_hbm.at[pl.ds(src, nc), :], xbuf.at[slot, pl.ds(dst, nc), :],
                                  rsem.at[slot]).start()
            return carry
        lax.fori_loop(0, blk, one, 0, unroll=16)

    def wait_rows(slot):
        pltpu.make_async_copy(h_hbm.at[pl.ds(0, blk * nc), :], xbuf.at[slot], rsem.at[slot]).wait()

    @pl.when(jnp.logical_and(i == 0, nact > 0))
    def _():
        idx_copy(0, 0).start()
        idx_copy(0, 0).wait()
        issue_rows(0)

        @pl.when(nact > 1)
        def _():
            idx_copy(1, 1).start()

    cur = lax.rem(i, 2)
    nxt = 1 - cur

    has_next = i + 1 < nact

    @pl.when(has_next)
    def _():
        idx_copy(i + 1, nxt).wait()

    @pl.when(i < nact)
    def _():
        wait_rows(cur)
        for j in range(blk):
            @pl.when(has_next)
            def _():
                src = pl.multiple_of(idx_sm[nxt, j], nc)
                pltpu.make_async_copy(h_hbm.at[pl.ds(src, nc), :], xbuf.at[nxt, pl.ds(j * nc, nc), :],
                                      rsem.at[nxt]).start()
        x = _load_token_tiles(xbuf.at[cur], nc).astype(BF16)
        ug = jnp.dot(x, w1g_ref[0], preferred_element_type=F32) + b1g_ref[0]
        ul = jnp.dot(x, w1l_ref[0], preferred_element_type=F32) + b1l_ref[0]
        x_glu = jnp.minimum(ug, SWIGLU_LIMIT)
        x_lin = jnp.clip(ul, -SWIGLU_LIMIT, SWIGLU_LIMIT)
        act = (x_lin + 1.0) * (x_glu * (1.0 / (1.0 + jnp.exp(-SWIGLU_ALPHA * x_glu))))
        y = jnp.dot(act.astype(BF16), w2_ref[0], preferred_element_type=F32) + b2_ref[0]
        _store_token_tiles(y_ref, y)

    @pl.when(i + 2 < nact)
    def _():
        idx_copy(i + 2, cur).start()

    @pl.when(i >= nact)
    def _():
        y_ref[...] = jnp.zeros(y_ref.shape, F32)


def _experts(h2, slot_tok, block_e, nact, p, nc):
    nb, blk = slot_tok.shape
    wspec = lambda a: pl.BlockSpec((1,) + a.shape[1:], lambda i, be, na: (be[i], 0, 0))
    grid_spec = pltpu.PrefetchScalarGridSpec(
        num_scalar_prefetch=2,
        grid=(nb,),
        in_specs=[pl.BlockSpec(memory_space=pl.ANY), pl.BlockSpec(memory_space=pl.ANY),
                  wspec(p['w1g']), wspec(p['w1l']), wspec(p['w2']),
                  wspec(p['b1g']), wspec(p['b1l']), wspec(p['b2'])],
        out_specs=pl.BlockSpec((blk * nc, LANE), lambda i, be, na: (i, 0)),
        scratch_shapes=[pltpu.SMEM((2, blk), jnp.int32), pltpu.VMEM((2, blk * nc, LANE), F32),
                        pltpu.SemaphoreType.DMA((2,)), pltpu.SemaphoreType.DMA((2,))],
    )
    return pl.pallas_call(
        functools.partial(_experts_kernel, blk=blk, nc=nc),
        grid_spec=grid_spec,
        out_shape=jax.ShapeDtypeStruct((nb * blk * nc, LANE), F32),
        compiler_params=_cparams(("arbitrary",)),
        name="experts",
    )(block_e, nact, slot_tok, h2, p['w1g'], p['w1l'], p['w2'], p['b1g'], p['b1l'], p['b2'])


def _deint_kernel(w_ref, sel_ref, g_ref, l_ref):
    sel = sel_ref[...]
    for blk in range(w_ref.shape[2] // (2 * LANE)):
        w = w_ref[0, :, blk * 2 * LANE:(blk + 1) * 2 * LANE].astype(BF16)
        r = jnp.dot(w, sel, preferred_element_type=F32).astype(BF16)
        g_ref[0, :, blk * LANE:(blk + 1) * LANE] = r[:, :LANE]
        l_ref[0, :, blk * LANE:(blk + 1) * LANE] = r[:, LANE:]


def _deinterleave_w1(w_mlp1):
    ne, d, f2 = w_mlp1.shape
    f = f2 // 2
    rows = min(512, d)
    src = lax.broadcasted_iota(jnp.int32, (2 * LANE, 2 * LANE), 0)
    dst = lax.broadcasted_iota(jnp.int32, (2 * LANE, 2 * LANE), 1)
    sel = jnp.where(dst < LANE, src == 2 * dst, src == 2 * (dst - LANE) + 1).astype(BF16)
    return pl.pallas_call(
        _deint_kernel,
        grid=(ne, d // rows),
        in_specs=[pl.BlockSpec((1, rows, f2), lambda e, r: (e, r, 0)),
                  pl.BlockSpec((2 * LANE, 2 * LANE), lambda e, r: (0, 0))],
        out_specs=[pl.BlockSpec((1, rows, f), lambda e, r: (e, r, 0))] * 2,
        out_shape=[jax.ShapeDtypeStruct((ne, d, f), BF16)] * 2,
        compiler_params=_cparams(("arbitrary", "arbitrary")),
        name="deint",
    )(w_mlp1, sel)


def _combine_kernel(dest_hbm, y_hbm, x1_ref, g_ref, ada_ref, ln_ref, o_ref,
                    idx_sm, ybuf, isem, rsem, *, t, nsteps, nc):
    i = pl.program_id(0) * pl.num_programs(1) + pl.program_id(1)

    def idx_copy(step, slot):
        return pltpu.make_async_copy(dest_hbm.at[step], idx_sm.at[slot], isem.at[slot])

    def issue_rows(slot):
        def one(j, carry):
            for kk in range(TOP_K):
                src = pl.multiple_of(idx_sm[slot, j * TOP_K + kk], nc)
                dst = pl.multiple_of(j * nc, nc)
                pltpu.make_async_copy(y_hbm.at[pl.ds(src, nc), :], ybuf.at[slot, kk, pl.ds(dst, nc), :],
                                      rsem.at[slot]).start()
            return carry
        lax.fori_loop(0, t, one, 0, unroll=4)

    def wait_rows(slot):
        for kk in range(TOP_K):
            pltpu.make_async_copy(y_hbm.at[pl.ds(0, t * nc), :], ybuf.at[slot, kk], rsem.at[slot]).wait()

    @pl.when(i == 0)
    def _():
        idx_copy(0, 0).start()
        idx_copy(0, 0).wait()
        issue_rows(0)
        if nsteps > 1:
            idx_copy(1, 1).start()

    cur = lax.rem(i, 2)
    nxt = 1 - cur

    has_next = i + 1 < nsteps

    @pl.when(has_next)
    def _():
        idx_copy(i + 1, nxt).wait()

    for j in range(t):
        for kk in range(TOP_K):
            @pl.when(has_next)
            def _():
                src = pl.multiple_of(idx_sm[nxt, j * TOP_K + kk], nc)
                pltpu.make_async_copy(y_hbm.at[pl.ds(src, nc), :], ybuf.at[nxt, kk, pl.ds(j * nc, nc), :],
                                      rsem.at[nxt]).start()

    @pl.when(i + 2 < nsteps)
    def _():
        idx_copy(i + 2, cur).start()

    wait_rows(cur)
    g = g_ref[0]
    y = jnp.zeros((t, x1_ref.shape[2]), F32)
    for kk in range(TOP_K):
        y = y + _load_token_tiles(ybuf.at[cur, kk], nc) * g[:, kk:kk + 1]
    g_m = ada_ref[0, 5:6, :]
    o_ref[0] = x1_ref[0] + g_m * (_rms(y) * ln_ref[...])


def _combine(dest, y_sorted, x1, gates, ada3, ln_post_moe):
    b, s, d = x1.shape
    t = min(T_COMB, s)
    ns = s // t
    nsteps = b * ns
    nc = d // LANE
    dest2 = (dest * nc).reshape(nsteps, t * TOP_K)
    return pl.pallas_call(
        functools.partial(_combine_kernel, t=t, nsteps=nsteps, nc=nc),
        grid=(b, ns),
        in_specs=[pl.BlockSpec(memory_space=pl.ANY), pl.BlockSpec(memory_space=pl.ANY),
                  pl.BlockSpec((1, t, d), lambda i, j: (i, j, 0)),
                  pl.BlockSpec((1, t, LANE), lambda i, j: (i, j, 0)),
                  pl.BlockSpec((1, 6, d), lambda i, j: (i, 0, 0)),
                  pl.BlockSpec(ln_post_moe.shape, lambda i, j: (0, 0))],
        out_specs=pl.BlockSpec((1, t, d), lambda i, j: (i, j, 0)),
        out_shape=jax.ShapeDtypeStruct((b, s, d), F32),
        scratch_shapes=[pltpu.SMEM((2, t * TOP_K), jnp.int32),
                        pltpu.VMEM((2, TOP_K, t * nc, LANE), F32),
                        pltpu.SemaphoreType.DMA((2,)), pltpu.SemaphoreType.DMA((2,))],
        compiler_params=_cparams(("arbitrary", "arbitrary")),
        name="combine",
    )(dest2, y_sorted, x1, gates, ada3, ln_post_moe)


def _rope_partner(w):
    half = MLA_ROPE // 2
    return jnp.concatenate([-w[..., half:], w[..., :half]], axis=-1)


def _prep_params(ln_pre_mix, ln_post_mix, ln_pre_moe, ln_post_moe, w_in, na_rpb, q_a_norm, w_q_b,
                 kv_a_norm, w_kv_b, g_na_out, g_mla_out, w_o, w_router, b_router,
                 w_mlp1, b_mlp1, w_mlp2, b_mlp2):
    d = w_in.shape[0]
    na_w = NA_HEADS * HEAD_DIM
    q_lora = w_q_b.shape[0]
    kv_lora = w_kv_b.shape[0]
    qk = MLA_NOPE + MLA_ROPE
    main_w = 3 * na_w + q_lora + kv_lora
    w_kr = w_in[:, main_w:main_w + MLA_ROPE]
    zpad = lambda n: jnp.zeros((d, n), F32)
    kra = jnp.concatenate([zpad(MLA_NOPE), w_kr, zpad(HEAD_PAD - qk)], axis=1)
    krb = jnp.concatenate([zpad(MLA_NOPE), _rope_partner(w_kr), zpad(HEAD_PAD - qk)], axis=1)
    w1 = jnp.concatenate([w_in[:, :main_w], kra, krb], axis=1).astype(BF16)

    wq = w_q_b.reshape(q_lora, MLA_HEADS, qk)
    zq = lambda n: jnp.zeros((q_lora, MLA_HEADS, n), F32)
    wqa = jnp.concatenate([wq, zq(HEAD_PAD - qk)], axis=2).reshape(q_lora, -1).astype(BF16)
    wqb = jnp.concatenate([zq(MLA_NOPE), _rope_partner(wq[:, :, MLA_NOPE:]), zq(HEAD_PAD - qk)],
                          axis=2).reshape(q_lora, -1).astype(BF16)
    wkv = w_kv_b.reshape(kv_lora, MLA_HEADS, MLA_NOPE + MLA_V)
    zk = lambda n: jnp.zeros((kv_lora, MLA_HEADS, n), F32)
    wk = jnp.concatenate([wkv[:, :, :MLA_NOPE], zk(HEAD_PAD - MLA_NOPE)], axis=2).reshape(kv_lora, -1).astype(BF16)
    wv = jnp.concatenate([wkv[:, :, MLA_NOPE:], zk(HEAD_PAD - MLA_V)], axis=2).reshape(kv_lora, -1).astype(BF16)

    mla_w = MLA_HEADS * MLA_V
    g_mla_pad = jnp.concatenate([g_mla_out.reshape(MLA_HEADS, MLA_V),
                                 jnp.zeros((MLA_HEADS, HEAD_PAD - MLA_V), F32)], axis=1).reshape(1, -1)
    wo_mla = w_o[na_w:na_w + mla_w].reshape(MLA_HEADS, MLA_V, d)
    wo_mla = jnp.concatenate([wo_mla, jnp.zeros((MLA_HEADS, HEAD_PAD - MLA_V, d), F32)],
                             axis=1).reshape(MLA_HEADS * HEAD_PAD, d).astype(BF16)
    w_r = jnp.concatenate([w_router, jnp.zeros((d, LANE - N_EXPERTS), F32)], axis=1)
    w_r_hi = w_r.astype(BF16)
    w_r_lo = (w_r - w_r_hi.astype(F32)).astype(BF16)
    w_r = jnp.concatenate([w_r_hi, w_r_hi, w_r_lo], axis=0)
    b_r = jnp.concatenate([b_router, jnp.zeros((LANE - N_EXPERTS,), F32)]).reshape(1, LANE)
    ne = w_mlp1.shape[0]
    w1g, w1l = _deinterleave_w1(w_mlp1)
    return {
        'ln_pre_mix': ln_pre_mix.reshape(1, d), 'ln_post_mix': ln_post_mix.reshape(1, d),
        'ln_pre_moe': ln_pre_moe.reshape(1, d), 'ln_post_moe': ln_post_moe.reshape(1, d),
        'w1': w1, 'q_a_norm': q_a_norm.reshape(1, q_lora), 'wqa': wqa, 'wqb': wqb,
        'kv_a_norm': kv_a_norm.reshape(1, kv_lora), 'wk': wk, 'wv': wv,
        'na_bias': _na_bias_table(na_rpb),
        'g_na_out': g_na_out.reshape(1, na_w), 'g_mla_pad': g_mla_pad,
        'wo_na': w_o[:na_w].astype(BF16), 'wo_mla': wo_mla,
        'w_router': w_r, 'b_router': b_r,
        'w1g': w1g, 'w1l': w1l,
        'w2': w_mlp2.astype(BF16),
        'b1g': b_mlp1[:, 0::2].reshape(ne, 1, -1), 'b1l': b_mlp1[:, 1::2].reshape(ne, 1, -1),
        'b2': b_mlp2.reshape(ne, 1, -1),
    }


def _rope_lane_tables(s):
    half = MLA_ROPE // 2
    inv = ROPE_THETA ** (-jnp.arange(half, dtype=F32) / half)
    ang = jnp.arange(s, dtype=F32)[:, None] * inv[None, :]
    cos, sin = jnp.cos(ang), jnp.sin(ang)
    z = lambda n: jnp.zeros((s, n), F32)
    tail = HEAD_PAD - MLA_NOPE - MLA_ROPE
    ck = jnp.concatenate([z(MLA_NOPE), cos, cos, z(tail)], axis=1)
    sk = jnp.concatenate([z(MLA_NOPE), sin, sin, z(tail)], axis=1)
    return ck, sk


def _moe_plan(e4, rank4, counts, n, nc):
    nk = n * TOP_K
    padded = (counts + MOE_BLK - 1) // MOE_BLK * MOE_BLK
    pad_end = jnp.cumsum(padded)
    pad_start = pad_end - padded
    dest = pad_start[e4] + rank4
    nb = (nk + N_EXPERTS * (MOE_BLK - 1) + MOE_BLK - 1) // MOE_BLK
    tok = jnp.broadcast_to(jnp.arange(n, dtype=jnp.int32)[:, None] * nc, (n, TOP_K))
    slot_tok = jnp.zeros((nb * MOE_BLK,), jnp.int32).at[dest.reshape(-1)].set(
        tok.reshape(-1), unique_indices=True, mode='promise_in_bounds')
    block_start = jnp.arange(nb, dtype=jnp.int32) * MOE_BLK
    block_e = jnp.minimum(jnp.sum((pad_end[None, :] <= block_start[:, None]).astype(jnp.int32), axis=1),
                          N_EXPERTS - 1)
    nact = (pad_end[-1] // MOE_BLK).astype(jnp.int32).reshape(1)
    return dest.astype(jnp.int32), slot_tok.reshape(nb, MOE_BLK), block_e, nact


def _encoder_layer(x, ada3, p):
    b, s, d = x.shape
    ck, sk = _rope_lane_tables(s)
    qna, kna, vna, q, k, v = _inproj(x, ada3, p, ck, sk)
    na_out = _na(qna, kna, vna, p['na_bias'])
    mla_out = _flash(q, k, v)
    x1, h2, e, g, rank, cnt = _post(na_out, mla_out, x, ada3, p)
    n = b * s
    e4 = e.reshape(n, LANE)[:, :TOP_K]
    rank4 = rank.reshape(n, LANE)[:, :TOP_K]
    counts = cnt[0, :N_EXPERTS].astype(jnp.int32)
    nc = d // LANE
    dest, slot_tok, block_e, nact = _moe_plan(e4, rank4, counts, n, nc)
    y_sorted = _experts(h2.reshape(n * nc, LANE), slot_tok, block_e, nact, p, nc)
    return _combine(dest, y_sorted, x1, g, ada3, p['ln_post_moe'])


def kernel(x_prompt, x_sample, c_prompt, c_sample, ln_pre_mix, ln_post_mix, ln_pre_moe, ln_post_moe, w_ada, b_ada, w_in, na_rpb, q_a_norm, w_q_b, kv_a_norm, w_kv_b, g_na_out, g_mla_out, w_o, w_router, b_router, w_mlp1, b_mlp1, w_mlp2, b_mlp2):
    depth = w_ada.shape[0]
    d = x_prompt.shape[-1]
    bp, bs = x_prompt.shape[0], x_sample.shape[0]
    rows = -(-(bp + bs) // 8) * 8
    y_prompt, y_sample = x_prompt, x_sample
    for l in range(depth):
        p = _prep_params(ln_pre_mix[l], ln_post_mix[l], ln_pre_moe[l], ln_post_moe[l], w_in[l], na_rpb[l],
                         q_a_norm[l], w_q_b[l], kv_a_norm[l], w_kv_b[l], g_na_out[l], g_mla_out[l], w_o[l],
                         w_router[l], b_router[l], w_mlp1[l], b_mlp1[l], w_mlp2[l], b_mlp2[l])
        c_all = jnp.concatenate([c_prompt, c_sample, jnp.zeros((rows - bp - bs, d), F32)], axis=0)
        ada3 = _ada(c_all, w_ada[l], b_ada[l]).reshape(rows, 6, d)
        y_prompt = _encoder_layer(y_prompt, ada3[:bp], p)
        y_sample = _encoder_layer(y_sample, ada3[bp:bp + bs], p)
    return (y_prompt, y_sample)
```
